```python
import jax, jax.numpy as jnp
from jax import lax
import numpy as np

D_MODEL = 4096
BATCH = 4
SEQ = 2048
DEPTH = 2
DEC_BATCH = 16
DEC_SEQ = 16
PAST_LEN = 2048

CHUNK = 64
Q_BLOCK = 128
HEAD_DIM = 128
ROPE_THETA = 500000.0
ROT_FRACTION = 4
NORM_EPS = 1e-6
A_HEADS = D_MODEL // 256
A_KV_HEADS = A_HEADS // 4
IDX_HEADS = D_MODEL // 128
IDX_DIM = 64
TOPK_MAX = 256
B_WIDTH = D_MODEL // 2
CONV_W = 3
C_HEADS = D_MODEL // 256
C_DK = 128
C_DV = 128
D_HEADS = D_MODEL // 256
FFN_HIDDEN = ((8 * D_MODEL + 3 * 256 - 1) // (3 * 256)) * 256
N_EVEN = (DEPTH + 1) // 2
N_ODD = DEPTH // 2
AB_SIZES = (A_HEADS * HEAD_DIM, A_KV_HEADS * HEAD_DIM, A_KV_HEADS * HEAD_DIM,
            IDX_HEADS * IDX_DIM, IDX_DIM, IDX_HEADS, B_WIDTH, B_WIDTH, B_WIDTH)
AB_IN = sum(AB_SIZES)
AB_OUT = A_HEADS * HEAD_DIM + B_WIDTH
CD_SIZES = (C_HEADS * C_DK, C_HEADS * C_DK, C_HEADS * C_DV, C_HEADS * C_DV,
            D_HEADS * HEAD_DIM, D_HEADS * HEAD_DIM, D_HEADS * HEAD_DIM, D_HEADS, D_HEADS * HEAD_DIM)
CD_IN = sum(CD_SIZES)
CD_OUT = C_HEADS * C_DV + D_HEADS * HEAD_DIM

kernel_name = 'hybrid_streaming_encoder_step'


def _split(z, sizes):
    offs = np.cumsum(np.array(sizes))[:-1].tolist()
    return jnp.split(z, offs, axis=-1)


def _blocks(a, nb, qb):
    return jnp.moveaxis(a.reshape((a.shape[0], nb, qb) + a.shape[2:]), 1, 0)


def rms_norm(x, g):
    xf = x.astype(jnp.float32)
    y = xf * lax.rsqrt(jnp.mean(xf * xf, axis=-1, keepdims=True) + NORM_EPS)
    return (y * g.astype(jnp.float32)).astype(x.dtype)


def partial_rope(x, pos):
    d = x.shape[-1]
    r = d // ROT_FRACTION
    half = r // 2
    inv = 1.0 / (ROPE_THETA ** (jnp.arange(half, dtype=jnp.float32) / half))
    ang = pos.astype(jnp.float32)[:, None] * inv[None, :]
    cos = jnp.cos(ang)[:, None, :]
    sin = jnp.sin(ang)[:, None, :]
    xf = x.astype(jnp.float32)
    x1 = xf[..., :half]
    x2 = xf[..., half:r]
    out = jnp.concatenate([x1 * cos - x2 * sin, x2 * cos + x1 * sin, xf[..., r:]], axis=-1)
    return out.astype(x.dtype)


def dsa_attention(q, k, v, qi, ki, wi, k_cache, v_cache, ki_cache, past_len):
    Bn, T = q.shape[:2]
    f32 = jnp.float32
    kf = jnp.concatenate([k_cache.astype(k.dtype), k], axis=1)
    vf = jnp.concatenate([v_cache.astype(v.dtype), v], axis=1)
    kif = jnp.concatenate([ki_cache.astype(ki.dtype), ki], axis=1)
    L = past_len + T
    n_sel = min(TOPK_MAX, L // 4)
    k_pos = jnp.arange(L)
    q_pos = past_len + jnp.arange(T)
    qb = min(Q_BLOCK, T)
    nb = T // qb
    grp = A_HEADS // A_KV_HEADS
    kif32 = kif.astype(f32)
    idx_scale = (IDX_HEADS * IDX_DIM) ** -0.5

    def one_block(args):
        q_b, qi_b, wi_b, qp = args
        s = jax.nn.relu(jnp.einsum('bthd,bsd->bths', qi_b.astype(f32), kif32))
        score = jnp.einsum('bths,bth->bts', s, wi_b.astype(f32)) * idx_scale
        adm = (k_pos[None, :] // CHUNK) <= (qp[:, None] // CHUNK)
        score = jnp.where(adm[None], score, -jnp.inf)
        _, idx = lax.top_k(score, n_sel)
        valid = (idx // CHUNK) <= (qp[None, :, None] // CHUNK)
        kg = jax.vmap(lambda kb, ib: kb[ib])(kf, idx).astype(f32)
        vg = jax.vmap(lambda vb, ib: vb[ib])(vf, idx).astype(f32)
        qg = q_b.reshape(Bn, qb, A_KV_HEADS, grp, HEAD_DIM).astype(f32)
        logits = jnp.einsum('btkgd,btskd->btkgs', qg, kg) * (HEAD_DIM ** -0.5)
        logits = jnp.where(valid[:, :, None, None, :], logits, -jnp.inf)
        p = jax.nn.softmax(logits, axis=-1)
        o = jnp.einsum('btkgs,btskd->btkgd', p, vg)
        return o.reshape(Bn, qb, A_HEADS * HEAD_DIM).astype(q.dtype)

    out = lax.map(one_block, (_blocks(q, nb, qb), _blocks(qi, nb, qb), _blocks(wi, nb, qb),
                              q_pos.reshape(nb, qb)))
    return jnp.moveaxis(out, 0, 1).reshape(Bn, T, A_HEADS * HEAD_DIM)


def short_conv(b_gate, c_gate, u_in, conv_w, conv_state):
    T = u_in.shape[1]
    u = c_gate * u_in
    full = jnp.concatenate([conv_state.astype(u.dtype), u], axis=1)
    y = full[:, 0:T] * conv_w[0]
    for j in range(1, CONV_W):
        y = y + full[:, j:j + T] * conv_w[j]
    return b_gate * y, full[:, T:]


def hgrn2(q, f_raw, i_in, g, lb, o_gain, s0):
    Bn, T, _ = q.shape

    def heads(a, d):
        return a.reshape(Bn, T, C_HEADS, d).transpose(0, 2, 1, 3).astype(jnp.float32)

    lbh = lb.reshape(C_HEADS, 1, C_DK)
    f = lbh + (1.0 - lbh) * jax.nn.sigmoid(heads(f_raw, C_DK))
    logf = jnp.log(f)
    k = 1.0 - f
    qh = jax.nn.silu(heads(q, C_DK))
    vh = heads(i_in, C_DV)
    tc = min(CHUNK, T)
    nc = T // tc

    def to_chunks(a):
        return jnp.moveaxis(a.reshape(Bn, C_HEADS, nc, tc, a.shape[-1]), 2, 0)

    tri = jnp.tril(jnp.ones((tc, tc), dtype=bool))[:, :, None]

    def step(S, xs):
        qc, lc, kc, vc = xs
        A = jnp.cumsum(lc, axis=2)
        o_inter = jnp.einsum('bhtd,bhdv->bhtv', qc * jnp.exp(A), S)
        expo = A[:, :, :, None, :] - A[:, :, None, :, :]
        dec = jnp.where(tri, jnp.exp(jnp.where(tri, expo, 0.0)), 0.0)
        sc = jnp.einsum('bhtd,bhtsd,bhsd->bhts', qc, dec, kc)
        o = o_inter + jnp.einsum('bhts,bhsv->bhtv', sc, vc)
        A_last = A[:, :, -1:, :]
        S_new = (jnp.exp(A_last[:, :, 0, :])[..., None] * S
                 + jnp.einsum('bhsd,bhsv->bhdv', kc * jnp.exp(A_last - A), vc))
        return S_new, o

    s_fin, o = lax.scan(step, s0.astype(jnp.float32),
                        (to_chunks(qh), to_chunks(logf), to_chunks(k), to_chunks(vh)))
    o = o.transpose(1, 2, 0, 3, 4).reshape(Bn, C_HEADS, T, C_DV).transpose(0, 2, 1, 3)
    o = rms_norm(o, o_gain) * jax.nn.silu(g.reshape(Bn, T, C_HEADS, C_DV).astype(jnp.float32))
    return o.reshape(Bn, T, C_HEADS * C_DV).astype(g.dtype), s_fin.astype(s0.dtype)


def forgetting_attention(q, k, v, f_logit, g, k_cache, v_cache, lf_cache, past_len):
    Bn, T = q.shape[:2]
    f32 = jnp.float32
    logf = jax.nn.log_sigmoid(f_logit.astype(f32))
    kf32 = jnp.concatenate([k_cache.astype(k.dtype), k], axis=1).astype(f32)
    vf32 = jnp.concatenate([v_cache.astype(v.dtype), v], axis=1).astype(f32)
    cum = jnp.cumsum(jnp.concatenate([lf_cache.astype(f32), logf], axis=1), axis=1)
    ck = cum.transpose(0, 2, 1)
    cq = cum[:, past_len:]
    L = past_len + T
    k_pos = jnp.arange(L)
    q_pos = past_len + jnp.arange(T)
    qb = min(Q_BLOCK, T)
    nb = T // qb

    def one_block(args):
        q_b, cq_b, qp = args
        logits = jnp.einsum('bqhd,bshd->bhqs', q_b.astype(f32), kf32) * (HEAD_DIM ** -0.5)
        logits = logits + cq_b.transpose(0, 2, 1)[..., None] - ck[:, :, None, :]
        mask = k_pos[None, :] <= qp[:, None]
        logits = jnp.where(mask[None, None], logits, -jnp.inf)
        p = jax.nn.softmax(logits, axis=-1)
        return jnp.einsum('bhqs,bshd->bqhd', p, vf32)

    out = lax.map(one_block, (_blocks(q, nb, qb), _blocks(cq, nb, qb), q_pos.reshape(nb, qb)))
    o = jnp.moveaxis(out, 0, 1).reshape(Bn, T, D_HEADS, HEAD_DIM)
    o = o * jax.nn.sigmoid(g.reshape(Bn, T, D_HEADS, HEAD_DIM).astype(f32))
    return o.reshape(Bn, T, D_HEADS * HEAD_DIM).astype(q.dtype), logf.astype(q.dtype)


def _trunk(x, c, past_len, a_k, a_v, a_ik, b_conv, c_s, d_k, d_v, d_lf,
           ada_w, ada_b, norm_mix, norm_ffn, ab_w_in, ab_w_out, a_q_norm, a_k_norm, b_conv_w,
           cd_w_in, cd_w_out, c_lb, c_o_norm, d_q_norm, d_k_norm, d_f_bias,
           ffn_w_gate, ffn_w_up, ffn_w_down):
    Bn, T, _ = x.shape
    pos = past_len + jnp.arange(T)
    lb_all = jnp.cumsum(jax.nn.softmax(c_lb.astype(jnp.float32), axis=0), axis=0)
    lb_all = lb_all - lb_all[0]
    na_k, na_v, na_ik, nb_conv, nc_s, nd_k, nd_v, nd_lf = [], [], [], [], [], [], [], []
    for l in range(DEPTH):
        mod = jax.nn.silu(c) @ ada_w[l] + ada_b[l]
        sh1, sc1, g1, sh2, sc2, g2 = [m[:, None, :] for m in jnp.split(mod, 6, axis=-1)]
        h = rms_norm(x, norm_mix[l]) * (1.0 + sc1) + sh1
        if l % 2 == 0:
            e = l // 2
            q, k, v, qi, ki, wi, bg, cg, bin_ = _split(h @ ab_w_in[e], AB_SIZES)
            q = partial_rope(rms_norm(q.reshape(Bn, T, A_HEADS, HEAD_DIM), a_q_norm[e]), pos)
            k = partial_rope(rms_norm(k.reshape(Bn, T, A_KV_HEADS, HEAD_DIM), a_k_norm[e]), pos)
            v = v.reshape(Bn, T, A_KV_HEADS, HEAD_DIM)
            qi = partial_rope(qi.reshape(Bn, T, IDX_HEADS, IDX_DIM), pos)
            ki = partial_rope(ki[:, :, None, :], pos)[:, :, 0, :]
            ya = dsa_attention(q, k, v, qi, ki, wi, a_k[e], a_v[e], a_ik[e], past_len)
            yb, conv_new = short_conv(bg, cg, bin_, b_conv_w[e], b_conv[e])
            mix = jnp.concatenate([ya, yb], axis=-1) @ ab_w_out[e]
            na_k.append(k)
            na_v.append(v)
            na_ik.append(ki)
            nb_conv.append(conv_new)
        else:
            od = l // 2
            qc, fc, ic, gc, qd, kd, vd, fd, gd = _split(h @ cd_w_in[od], CD_SIZES)
            yc, s_new = hgrn2(qc, fc, ic, gc, lb_all[l], c_o_norm[od], c_s[od])
            qd = rms_norm(qd.reshape(Bn, T, D_HEADS, HEAD_DIM), d_q_norm[od])
            kd = rms_norm(kd.reshape(Bn, T, D_HEADS, HEAD_DIM), d_k_norm[od])
            vd = vd.reshape(Bn, T, D_HEADS, HEAD_DIM)
            yd, lf_new = forgetting_attention(qd, kd, vd, fd + d_f_bias[od], gd,
                                              d_k[od], d_v[od], d_lf[od], past_len)
            mix = jnp.concatenate([yc, yd], axis=-1) @ cd_w_out[od]
            nc_s.append(s_new)
            nd_k.append(kd)
            nd_v.append(vd)
            nd_lf.append(lf_new)
        x = x + g1 * mix
        h2 = rms_norm(x, norm_ffn[l]) * (1.0 + sc2) + sh2
        x = x + g2 * ((jax.nn.silu(h2 @ ffn_w_gate[l]) * (h2 @ ffn_w_up[l])) @ ffn_w_down[l])
    return (x, jnp.stack(na_k), jnp.stack(na_v), jnp.stack(na_ik), jnp.stack(nb_conv),
            jnp.stack(nc_s), jnp.stack(nd_k), jnp.stack(nd_v), jnp.stack(nd_lf))


def setup_inputs(seed: int = 0) -> dict:
    key = jax.random.key(seed)
    ks = iter(jax.random.split(key, 40))
    f32 = jnp.float32

    def nrm(shape, scale):
        return jax.random.normal(next(ks), shape, f32) * scale

    def gain(shape):
        return 1.0 + nrm(shape, 0.05)

    D = D_MODEL
    return {
        'x_prompt': nrm((BATCH, SEQ, D), 1.0),
        'x_sample': nrm((DEC_BATCH, DEC_SEQ, D), 1.0),
        'c_prompt': nrm((BATCH, D), 1.0),
        'c_sample': nrm((DEC_BATCH, D), 1.0),
        'cache_a_k': nrm((N_EVEN, DEC_BATCH, PAST_LEN, A_KV_HEADS, HEAD_DIM), 1.0),
        'cache_a_v': nrm((N_EVEN, DEC_BATCH, PAST_LEN, A_KV_HEADS, HEAD_DIM), 1.0),
        'cache_a_idx_k': nrm((N_EVEN, DEC_BATCH, PAST_LEN, IDX_DIM), 1.0),
        'state_b_conv': nrm((N_EVEN, DEC_BATCH, CONV_W - 1, B_WIDTH), 1.0),
        'state_c_s': nrm((N_ODD, DEC_BATCH, C_HEADS, C_DK, C_DV), 0.5),
        'cache_d_k': nrm((N_ODD, DEC_BATCH, PAST_LEN, D_HEADS, HEAD_DIM), 1.0),
        'cache_d_v': nrm((N_ODD, DEC_BATCH, PAST_LEN, D_HEADS, HEAD_DIM), 1.0),
        'cache_d_logf': jax.nn.log_sigmoid(3.0 + nrm((N_ODD, DEC_BATCH, PAST_LEN, D_HEADS), 0.5)),
        'ada_w': nrm((DEPTH, D, 6 * D), 0.5 * D ** -0.5),
        'ada_b': nrm((DEPTH, 6 * D), 0.02),
        'norm_mix': gain((DEPTH, D)),
        'norm_ffn': gain((DEPTH, D)),
        'ab_w_in': nrm((N_EVEN, D, AB_IN), D ** -0.5),
        'ab_w_out': nrm((N_EVEN, AB_OUT, D), AB_OUT ** -0.5),
        'a_q_norm': gain((N_EVEN, HEAD_DIM)),
        'a_k_norm': gain((N_EVEN, HEAD_DIM)),
        'b_conv_w': nrm((N_EVEN, CONV_W, B_WIDTH), CONV_W ** -0.5),
        'cd_w_in': nrm((N_ODD, D, CD_IN), D ** -0.5),
        'cd_w_out': nrm((N_ODD, CD_OUT, D), CD_OUT ** -0.5),
        'c_lb': nrm((DEPTH, C_HEADS * C_DK), 1.0),
        'c_o_norm': gain((N_ODD, C_DV)),
        'd_q_norm': gain((N_ODD, HEAD_DIM)),
        'd_k_norm': gain((N_ODD, HEAD_DIM)),
        'd_f_bias': 3.0 + nrm((N_ODD, D_HEADS), 0.1),
        'ffn_w_gate': nrm((DEPTH, D, FFN_HIDDEN), D ** -0.5),
        'ffn_w_up': nrm((DEPTH, D, FFN_HIDDEN), D ** -0.5),
        'ffn_w_down': nrm((DEPTH, FFN_HIDDEN, D), FFN_HIDDEN ** -0.5),
    }


def reference(x_prompt, x_sample, c_prompt, c_sample, cache_a_k, cache_a_v, cache_a_idx_k,
              state_b_conv, state_c_s, cache_d_k, cache_d_v, cache_d_logf,
              ada_w, ada_b, norm_mix, norm_ffn, ab_w_in, ab_w_out, a_q_norm, a_k_norm, b_conv_w,
              cd_w_in, cd_w_out, c_lb, c_o_norm, d_q_norm, d_k_norm, d_f_bias,
              ffn_w_gate, ffn_w_up, ffn_w_down):
    weights = (ada_w, ada_b, norm_mix, norm_ffn, ab_w_in, ab_w_out, a_q_norm, a_k_norm, b_conv_w,
               cd_w_in, cd_w_out, c_lb, c_o_norm, d_q_norm, d_k_norm, d_f_bias,
               ffn_w_gate, ffn_w_up, ffn_w_down)
    dt = x_prompt.dtype
    bp = x_prompt.shape[0]
    (y_prompt, p_a_k, p_a_v, p_a_ik, p_b_conv, p_c_s, p_d_k, p_d_v, p_d_lf) = _trunk(
        x_prompt, c_prompt, 0,
        jnp.zeros((N_EVEN, bp, 0, A_KV_HEADS, HEAD_DIM), dt),
        jnp.zeros((N_EVEN, bp, 0, A_KV_HEADS, HEAD_DIM), dt),
        jnp.zeros((N_EVEN, bp, 0, IDX_DIM), dt),
        jnp.zeros((N_EVEN, bp, CONV_W - 1, B_WIDTH), dt),
        jnp.zeros((N_ODD, bp, C_HEADS, C_DK, C_DV), dt),
        jnp.zeros((N_ODD, bp, 0, D_HEADS, HEAD_DIM), dt),
        jnp.zeros((N_ODD, bp, 0, D_HEADS, HEAD_DIM), dt),
        jnp.zeros((N_ODD, bp, 0, D_HEADS), dt),
        *weights)
    past_len = cache_a_k.shape[2]
    (y_sample, s_a_k, s_a_v, s_a_ik, s_b_conv, s_c_s, s_d_k, s_d_v, s_d_lf) = _trunk(
        x_sample, c_sample, past_len, cache_a_k, cache_a_v, cache_a_idx_k, state_b_conv,
        state_c_s, cache_d_k, cache_d_v, cache_d_logf, *weights)
    return (y_prompt, y_sample, p_a_k, p_a_v, p_a_ik, p_b_conv, p_c_s, p_d_k, p_d_v, p_d_lf,
            s_a_k, s_a_v, s_a_ik, s_b_conv, s_c_s, s_d_k, s_d_v, s_d_lf)
```

```python
import functools
import math

import numpy as np
import jax
import jax.numpy as jnp
from jax import lax
from jax.experimental import pallas as pl
from jax.experimental.pallas import tpu as pltpu

F32 = jnp.float32
BF16 = jnp.bfloat16

D_MODEL = 4096
CHUNK = 64
Q_BLOCK = 128
HEAD_DIM = 128
ROPE_THETA = 500000.0
ROT_FRACTION = 4
NORM_EPS = 1e-6
A_HEADS = D_MODEL // 256
A_KV_HEADS = A_HEADS // 4
IDX_HEADS = D_MODEL // 128
IDX_DIM = 64
TOPK_MAX = 256
B_WIDTH = D_MODEL // 2
CONV_W = 3
C_HEADS = D_MODEL // 256
C_DK = 128
C_DV = 128
D_HEADS = D_MODEL // 256
AB_SIZES = (A_HEADS * HEAD_DIM, A_KV_HEADS * HEAD_DIM, A_KV_HEADS * HEAD_DIM,
            IDX_HEADS * IDX_DIM, IDX_DIM, IDX_HEADS, B_WIDTH, B_WIDTH, B_WIDTH)
CD_SIZES = (C_HEADS * C_DK, C_HEADS * C_DK, C_HEADS * C_DV, C_HEADS * C_DV,
            D_HEADS * HEAD_DIM, D_HEADS * HEAD_DIM, D_HEADS * HEAD_DIM, D_HEADS, D_HEADS * HEAD_DIM)

LANES = 128
SUBLANES = 8
VMEM_LIMIT_BYTES = 56 * 1024 * 1024

ROW_GROUP = 16
N_ALIGN = 512


def _cparams(sem):
    return pltpu.CompilerParams(dimension_semantics=sem, vmem_limit_bytes=VMEM_LIMIT_BYTES)


def _pick_tile(n, candidates):
    for c in candidates:
        if n % c == 0:
            return c
    raise ValueError(f"no tile for {n}")


def _ada_kernel(c_ref, w_ref, b_ref, o_ref):
    c = c_ref[...]
    a = (c * jax.nn.sigmoid(c)).astype(BF16)
    o_ref[...] = jnp.dot(a, w_ref[...].astype(BF16), preferred_element_type=F32) + b_ref[...]


def _ada_mod(c, w, b):
    r, d = c.shape
    n = w.shape[1]
    tn = _pick_tile(n, (512, 256, 128))
    return pl.pallas_call(
        _ada_kernel,
        grid=(n // tn,),
        in_specs=[pl.BlockSpec((r, d), lambda j: (0, 0)),
                  pl.BlockSpec((d, tn), lambda j: (0, j)),
                  pl.BlockSpec((1, tn), lambda j: (0, j))],
        out_specs=pl.BlockSpec((r, tn), lambda j: (0, j)),
        out_shape=jax.ShapeDtypeStruct((r, n), F32),
        compiler_params=_cparams(("parallel",)),
        name="ada_mod",
    )(c, w, b)


def _norm_mod_kernel(x_ref, gain_ref, sc_ref, sh_ref, o_ref, *, groups):
    gain = gain_ref[...]
    for g in range(groups):
        rows = slice(g * ROW_GROUP, (g + 1) * ROW_GROUP)
        x = x_ref[rows, :]
        ms = jnp.mean(x * x, axis=-1, keepdims=True)
        y = x * lax.rsqrt(ms + NORM_EPS) * gain
        o_ref[rows, :] = (y * (1.0 + sc_ref[g:g + 1, :]) + sh_ref[g:g + 1, :]).astype(o_ref.dtype)


def _norm_mod(x, gain, sc, sh):
    m, d = x.shape
    tm = _pick_tile(m, (384, 256, 128))
    groups = tm // ROW_GROUP
    return pl.pallas_call(
        functools.partial(_norm_mod_kernel, groups=groups),
        grid=(m // tm,),
        in_specs=[pl.BlockSpec((tm, d), lambda i: (i, 0)),
                  pl.BlockSpec((1, d), lambda i: (0, 0)),
                  pl.BlockSpec((groups, d), lambda i: (i, 0)),
                  pl.BlockSpec((groups, d), lambda i: (i, 0))],
        out_specs=pl.BlockSpec((tm, d), lambda i: (i, 0)),
        out_shape=jax.ShapeDtypeStruct((m, d), BF16),
        compiler_params=_cparams(("parallel",)),
        name="norm_mod",
    )(x, gain, sc, sh)


def _mm_kernel(a_ref, w_ref, o_ref):
    o_ref[...] = jnp.dot(a_ref[...], w_ref[...], preferred_element_type=F32).astype(o_ref.dtype)


def _matmul(a, w, out_dtype=F32):
    m, k = a.shape
    n = w.shape[1]
    tm = _pick_tile(m, (768, 512, 256))
    tn = _pick_tile(n, (512, 256, 128))
    return pl.pallas_call(
        _mm_kernel,
        grid=(m // tm, n // tn),
        in_specs=[pl.BlockSpec((tm, k), lambda i, j: (i, 0)),
                  pl.BlockSpec((k, tn), lambda i, j: (0, j))],
        out_specs=pl.BlockSpec((tm, tn), lambda i, j: (i, j)),
        out_shape=jax.ShapeDtypeStruct((m, n), out_dtype),
        compiler_params=_cparams(("parallel", "parallel")),
        name="proj_in",
    )(a, w)


def _mm_resid_kernel(a_ref, w_ref, x_ref, g_ref, o_ref, acc_ref, *, nk, groups):
    k = pl.program_id(2)
    part = jnp.dot(a_ref[...], w_ref[...], preferred_element_type=F32)

    def finish(acc):
        for g in range(groups):
            rows = slice(g * ROW_GROUP, (g + 1) * ROW_GROUP)
            o_ref[rows, :] = x_ref[rows, :] + g_ref[g:g + 1, :] * acc[rows, :]

    if nk == 1:
        finish(part)
    else:
        @pl.when(k == 0)
        def _():
            acc_ref[...] = part

        @pl.when(jnp.logical_and(k > 0, k < nk - 1))
        def _():
            acc_ref[...] += part

        @pl.when(k == nk - 1)
        def _():
            finish(acc_ref[...] + part)


def _matmul_resid(a, w, x, gate):
    m, k = a.shape
    n = w.shape[1]
    tm = _pick_tile(m, (768, 512, 256))
    tn = _pick_tile(n, (512, 256, 128))
    tk = k if k <= 4096 else _pick_tile(k, (5504, 4096, 2048, 1024))
    nk = k // tk
    groups = tm // ROW_GROUP
    return pl.pallas_call(
        functools.partial(_mm_resid_kernel, nk=nk, groups=groups),
        grid=(m // tm, n // tn, nk),
        in_specs=[pl.BlockSpec((tm, tk), lambda i, j, kk: (i, kk)),
                  pl.BlockSpec((tk, tn), lambda i, j, kk: (kk, j)),
                  pl.BlockSpec((tm, tn), lambda i, j, kk: (i, j)),
                  pl.BlockSpec((groups, tn), lambda i, j, kk: (i, j))],
        out_specs=pl.BlockSpec((tm, tn), lambda i, j, kk: (i, j)),
        out_shape=jax.ShapeDtypeStruct((m, n), F32),
        scratch_shapes=[pltpu.VMEM((tm, tn), F32)],
        compiler_params=_cparams(("parallel", "parallel", "arbitrary")),
        name="proj_out_resid",
    )(a, w, x, gate)


def _ffn_gate_up_kernel(a_ref, wg_ref, wu_ref, o_ref):
    a = a_ref[...]
    g = jnp.dot(a, wg_ref[...], preferred_element_type=F32)
    u = jnp.dot(a, wu_ref[...], preferred_element_type=F32)
    o_ref[...] = (g * jax.nn.sigmoid(g) * u).astype(o_ref.dtype)


def _ffn_gate_up(a, wg, wu):
    m, k = a.shape
    n = wg.shape[1]
    tm = _pick_tile(m, (768, 512, 256))
    tn = _pick_tile(n, (512, 256, 128))
    return pl.pallas_call(
        _ffn_gate_up_kernel,
        grid=(m // tm, n // tn),
        in_specs=[pl.BlockSpec((tm, k), lambda i, j: (i, 0)),
                  pl.BlockSpec((k, tn), lambda i, j: (0, j)),
                  pl.BlockSpec((k, tn), lambda i, j: (0, j))],
        out_specs=pl.BlockSpec((tm, tn), lambda i, j: (i, j)),
        out_shape=jax.ShapeDtypeStruct((m, n), BF16),
        compiler_params=_cparams(("parallel", "parallel")),
        name="ffn_gate_up",
    )(a, wg, wu)


def _rms(x, g):
    xf = x.astype(F32)
    y = xf * lax.rsqrt(jnp.mean(xf * xf, axis=-1, keepdims=True) + NORM_EPS)
    return y * g.astype(F32)


def _rope(x, pos):
    d = x.shape[-1]
    r = d // ROT_FRACTION
    half = r // 2
    inv = 1.0 / (ROPE_THETA ** (jnp.arange(half, dtype=F32) / half))
    ang = pos.astype(F32)[:, None] * inv[None, :]
    cos = jnp.cos(ang)[:, None, :]
    sin = jnp.sin(ang)[:, None, :]
    x1 = x[..., :half]
    x2 = x[..., half:r]
    return jnp.concatenate([x1 * cos - x2 * sin, x2 * cos + x1 * sin, x[..., r:]], axis=-1)


def _blocks(a, nb, qb):
    return jnp.moveaxis(a.reshape((a.shape[0], nb, qb) + a.shape[2:]), 1, 0)


def _dsa_attention(q, k, v, qi, ki, wi, k_cache, v_cache, ki_cache, past_len):
    Bn, T = q.shape[:2]
    kf = jnp.concatenate([k_cache, k], axis=1)
    vf = jnp.concatenate([v_cache, v], axis=1)
    kif = jnp.concatenate([ki_cache, ki], axis=1)
    L = past_len + T
    n_sel = min(TOPK_MAX, L // 4)
    k_pos = jnp.arange(L)
    q_pos = past_len + jnp.arange(T)
    qb = min(Q_BLOCK, T)
    nb = T // qb
    grp = A_HEADS // A_KV_HEADS
    idx_scale = (IDX_HEADS * IDX_DIM) ** -0.5

    def one_block(args):
        q_b, qi_b, wi_b, qp = args
        s = jax.nn.relu(jnp.einsum('bthd,bsd->bths', qi_b, kif))
        score = jnp.einsum('bths,bth->bts', s, wi_b) * idx_scale
        adm = (k_pos[None, :] // CHUNK) <= (qp[:, None] // CHUNK)
        score = jnp.where(adm[None], score, -jnp.inf)
        _, idx = lax.top_k(score, n_sel)
        valid = (idx // CHUNK) <= (qp[None, :, None] // CHUNK)
        kg = jax.vmap(lambda kb, ib: kb[ib])(kf, idx)
        vg = jax.vmap(lambda vb, ib: vb[ib])(vf, idx)
        qg = q_b.reshape(Bn, qb, A_KV_HEADS, grp, HEAD_DIM)
        logits = jnp.einsum('btkgd,btskd->btkgs', qg, kg) * (HEAD_DIM ** -0.5)
        logits = jnp.where(valid[:, :, None, None, :], logits, -jnp.inf)
        p = jax.nn.softmax(logits, axis=-1)
        o = jnp.einsum('btkgs,btskd->btkgd', p, vg)
        return o.reshape(Bn, qb, A_HEADS * HEAD_DIM)

    out = lax.map(one_block, (_blocks(q, nb, qb), _blocks(qi, nb, qb), _blocks(wi, nb, qb),
                              q_pos.reshape(nb, qb)))
    return jnp.moveaxis(out, 0, 1).reshape(Bn, T, A_HEADS * HEAD_DIM)


def _short_conv(b_gate, c_gate, u_in, conv_w, conv_state):
    T = u_in.shape[1]
    u = c_gate * u_in
    full = jnp.concatenate([conv_state, u], axis=1)
    y = full[:, 0:T] * conv_w[0]
    for j in range(1, CONV_W):
        y = y + full[:, j:j + T] * conv_w[j]
    return b_gate * y, full[:, T:]


def _hgrn2(q, f_raw, i_in, g, lb, o_gain, s0):
    Bn, T, _ = q.shape

    def heads(a, d):
        return a.reshape(Bn, T, C_HEADS, d).transpose(0, 2, 1, 3)

    lbh = lb.reshape(C_HEADS, 1, C_DK)
    f = lbh + (1.0 - lbh) * jax.nn.sigmoid(heads(f_raw, C_DK))
    logf = jnp.log(f)
    k = 1.0 - f
    qh = jax.nn.silu(heads(q, C_DK))
    vh = heads(i_in, C_DV)
    tc = min(CHUNK, T)
    nc = T // tc

    def to_chunks(a):
        return jnp.moveaxis(a.reshape(Bn, C_HEADS, nc, tc, a.shape[-1]), 2, 0)

    tri = jnp.tril(jnp.ones((tc, tc), dtype=bool))[:, :, None]

    def step(S, xs):
        qc, lc, kc, vc = xs
        A = jnp.cumsum(lc, axis=2)
        o_inter = jnp.einsum('bhtd,bhdv->bhtv', qc * jnp.exp(A), S)
        expo = A[:, :, :, None, :] - A[:, :, None, :, :]
        dec = jnp.where(tri, jnp.exp(jnp.where(tri, expo, 0.0)), 0.0)
        sc = jnp.einsum('bhtd,bhtsd,bhsd->bhts', qc, dec, kc)
        o = o_inter + jnp.einsum('bhts,bhsv->bhtv', sc, vc)
        A_last = A[:, :, -1:, :]
        S_new = (jnp.exp(A_last[:, :, 0, :])[..., None] * S
                 + jnp.einsum('bhsd,bhsv->bhdv', kc * jnp.exp(A_last - A), vc))
        return S_new, o

    s_fin, o = lax.scan(step, s0, (to_chunks(qh), to_chunks(logf), to_chunks(k), to_chunks(vh)))
    o = o.transpose(1, 2, 0, 3, 4).reshape(Bn, C_HEADS, T, C_DV).transpose(0, 2, 1, 3)
    o = _rms(o, o_gain) * jax.nn.silu(g.reshape(Bn, T, C_HEADS, C_DV))
    return o.reshape(Bn, T, C_HEADS * C_DV), s_fin


def _fox_attention(q, k, v, f_logit, g, k_cache, v_cache, lf_cache, past_len):
    Bn, T = q.shape[:2]
    logf = jax.nn.log_sigmoid(f_logit)
    kf32 = jnp.concatenate([k_cache, k], axis=1)
    vf32 = jnp.concatenate([v_cache, v], axis=1)
    cum = jnp.cumsum(jnp.concatenate([lf_cache, logf], axis=1), axis=1)
    ck = cum.transpose(0, 2, 1)
    cq = cum[:, past_len:]
    L = past_len + T
    k_pos = jnp.arange(L)
    q_pos = past_len + jnp.arange(T)
    qb = min(Q_BLOCK, T)
    nb = T // qb

    def one_block(args):
        q_b, cq_b, qp = args
        logits = jnp.einsum('bqhd,bshd->bhqs', q_b, kf32) * (HEAD_DIM ** -0.5)
        logits = logits + cq_b.transpose(0, 2, 1)[..., None] - ck[:, :, None, :]
        mask = k_pos[None, :] <= qp[:, None]
        logits = jnp.where(mask[None, None], logits, -jnp.inf)
        p = jax.nn.softmax(logits, axis=-1)
        return jnp.einsum('bhqs,bshd->bqhd', p, vf32)

    out = lax.map(one_block, (_blocks(q, nb, qb), _blocks(cq, nb, qb), q_pos.reshape(nb, qb)))
    o = jnp.moveaxis(out, 0, 1).reshape(Bn, T, D_HEADS, HEAD_DIM)
    o = o * jax.nn.sigmoid(g.reshape(Bn, T, D_HEADS, HEAD_DIM))
    return o.reshape(Bn, T, D_HEADS * HEAD_DIM), logf


def _pad_cols(w, n):
    return jnp.pad(w, ((0, 0), (0, n - w.shape[1])))


def _ab_in_weight(w):
    offs = np.cumsum((0,) + AB_SIZES)
    main = jnp.concatenate([w[:, :offs[4]], w[:, offs[6]:]], axis=1)
    tail = _pad_cols(w[:, offs[4]:offs[6]], N_ALIGN)
    return jnp.concatenate([main, tail], axis=1).astype(BF16)


def _cd_in_weight(w):
    offs = np.cumsum((0,) + CD_SIZES)
    main = jnp.concatenate([w[:, :offs[7]], w[:, offs[8]:]], axis=1)
    tail = _pad_cols(w[:, offs[7]:offs[8]], N_ALIGN)
    return jnp.concatenate([main, tail], axis=1).astype(BF16)


def _expand_groups(mod_p, mod_s, seq_p):
    return jnp.concatenate([jnp.repeat(mod_p, seq_p // ROW_GROUP, axis=0), mod_s], axis=0)


def kernel(x_prompt, x_sample, c_prompt, c_sample, cache_a_k, cache_a_v, cache_a_idx_k, state_b_conv, state_c_s, cache_d_k, cache_d_v, cache_d_logf, ada_w, ada_b, norm_mix, norm_ffn, ab_w_in, ab_w_out, a_q_norm, a_k_norm, b_conv_w, cd_w_in, cd_w_out, c_lb, c_o_norm, d_q_norm, d_k_norm, d_f_bias, ffn_w_gate, ffn_w_up, ffn_w_down):
    bp, tp, d = x_prompt.shape
    bs, ts, _ = x_sample.shape
    past = cache_a_k.shape[2]
    depth = ada_w.shape[0]
    assert ts == ROW_GROUP and tp % ROW_GROUP == 0 and d == D_MODEL
    mp, ms = bp * tp, bs * ts

    x = jnp.concatenate([x_prompt.reshape(mp, d), x_sample.reshape(ms, d)], axis=0)
    c_all = jnp.concatenate([c_prompt, c_sample], axis=0)
    c_rows = -(-c_all.shape[0] // SUBLANES) * SUBLANES
    c_pad = jnp.pad(c_all, ((0, c_rows - c_all.shape[0]), (0, 0)))

    lb_all = jnp.cumsum(jax.nn.softmax(c_lb.astype(F32), axis=0), axis=0)
    lb_all = lb_all - lb_all[0]

    def split_streams(z):
        return z[:mp].reshape(bp, tp, -1), z[mp:].reshape(bs, ts, -1)

    outs_p, outs_s = {}, {}
    for l in range(depth):
        mod = _ada_mod(c_pad, ada_w[l], ada_b[l][None, :])
        mods = [_expand_groups(m[:bp], m[bp:bp + bs], tp) for m in jnp.split(mod, 6, axis=-1)]
        sh1, sc1, g1, sh2, sc2, g2 = mods
        h = _norm_mod(x, norm_mix[l][None, :], sc1, sh1)
        if l % 2 == 0:
            e = l // 2
            z = _matmul(h, _ab_in_weight(ab_w_in[e]))
            n_q, n_kv, n_qi = AB_SIZES[0], AB_SIZES[1], AB_SIZES[3]
            o = 0
            zq = z[:, o:o + n_q]; o += n_q
            zk = z[:, o:o + n_kv]; o += n_kv
            zv = z[:, o:o + n_kv]; o += n_kv
            zqi = z[:, o:o + n_qi]; o += n_qi
            zbg = z[:, o:o + B_WIDTH]; o += B_WIDTH
            zcg = z[:, o:o + B_WIDTH]; o += B_WIDTH
            zbin = z[:, o:o + B_WIDTH]; o += B_WIDTH
            zki = z[:, o:o + IDX_DIM]
            zwi = z[:, o + IDX_DIM:o + IDX_DIM + IDX_HEADS]
            ys = []
            for (sl, bn, t, pl_, kc, vc, ikc, cs, od) in (
                    (slice(0, mp), bp, tp, 0, None, None, None, None, outs_p),
                    (slice(mp, mp + ms), bs, ts, past, cache_a_k[e], cache_a_v[e], cache_a_idx_k[e],
                     state_b_conv[e], outs_s)):
                pos = pl_ + jnp.arange(t)
                if kc is None:
                    kc = jnp.zeros((bn, 0, A_KV_HEADS, HEAD_DIM), F32)
                    vc = jnp.zeros((bn, 0, A_KV_HEADS, HEAD_DIM), F32)
                    ikc = jnp.zeros((bn, 0, IDX_DIM), F32)
                    cs = jnp.zeros((bn, CONV_W - 1, B_WIDTH), F32)
                q = _rope(_rms(zq[sl].reshape(bn, t, A_HEADS, HEAD_DIM), a_q_norm[e]), pos)
                k = _rope(_rms(zk[sl].reshape(bn, t, A_KV_HEADS, HEAD_DIM), a_k_norm[e]), pos)
                v = zv[sl].reshape(bn, t, A_KV_HEADS, HEAD_DIM)
                qi = _rope(zqi[sl].reshape(bn, t, IDX_HEADS, IDX_DIM), pos)
                ki = _rope(zki[sl].reshape(bn, t, 1, IDX_DIM), pos)[:, :, 0, :]
                wi = zwi[sl].reshape(bn, t, IDX_HEADS)
                ya = _dsa_attention(q, k, v, qi, ki, wi, kc, vc, ikc, pl_)
                yb, conv_new = _short_conv(zbg[sl].reshape(bn, t, B_WIDTH), zcg[sl].reshape(bn, t, B_WIDTH),
                                           zbin[sl].reshape(bn, t, B_WIDTH), b_conv_w[e], cs)
                ys.append(jnp.concatenate([ya, yb], axis=-1).reshape(bn * t, -1))
                od.setdefault("a_k", []).append(k)
                od.setdefault("a_v", []).append(v)
                od.setdefault("a_ik", []).append(ki)
                od.setdefault("b_conv", []).append(conv_new)
            y = jnp.concatenate(ys, axis=0).astype(BF16)
            x = _matmul_resid(y, ab_w_out[e].astype(BF16), x, g1)
        else:
            od_ = l // 2
            z = _matmul(h, _cd_in_weight(cd_w_in[od_]))
            w = C_HEADS * C_DK
            zqc, zfc, zic, zgc, zqd, zkd, zvd, zgd = [z[:, i * w:(i + 1) * w] for i in range(8)]
            zfd = z[:, 8 * w:8 * w + D_HEADS]
            ys = []
            for (sl, bn, t, pl_, s0, kc, vc, lfc, od) in (
                    (slice(0, mp), bp, tp, 0, None, None, None, None, outs_p),
                    (slice(mp, mp + ms), bs, ts, past, state_c_s[od_], cache_d_k[od_], cache_d_v[od_],
                     cache_d_logf[od_], outs_s)):
                if s0 is None:
                    s0 = jnp.zeros((bn, C_HEADS, C_DK, C_DV), F32)
                    kc = jnp.zeros((bn, 0, D_HEADS, HEAD_DIM), F32)
                    vc = jnp.zeros((bn, 0, D_HEADS, HEAD_DIM), F32)
                    lfc = jnp.zeros((bn, 0, D_HEADS), F32)
                r3 = lambda a: a[sl].reshape(bn, t, -1)
                yc, s_new = _hgrn2(r3(zqc), r3(zfc), r3(zic), r3(zgc), lb_all[l], c_o_norm[od_], s0)
                qd = _rms(r3(zqd).reshape(bn, t, D_HEADS, HEAD_DIM), d_q_norm[od_])
                kd = _rms(r3(zkd).reshape(bn, t, D_HEADS, HEAD_DIM), d_k_norm[od_])
                vd = r3(zvd).reshape(bn, t, D_HEADS, HEAD_DIM)
                yd, lf_new = _fox_attention(qd, kd, vd, r3(zfd) + d_f_bias[od_], r3(zgd), kc, vc, lfc, pl_)
                ys.append(jnp.concatenate([yc, yd], axis=-1).reshape(bn * t, -1))
                od.setdefault("c_s", []).append(s_new)
                od.setdefault("d_k", []).append(kd)
                od.setdefault("d_v", []).append(vd)
                od.setdefault("d_lf", []).append(lf_new)
            y = jnp.concatenate(ys, axis=0).astype(BF16)
            x = _matmul_resid(y, cd_w_out[od_].astype(BF16), x, g1)
        h2 = _norm_mod(x, norm_ffn[l][None, :], sc2, sh2)
        hid = _ffn_gate_up(h2, ffn_w_gate[l].astype(BF16), ffn_w_up[l].astype(BF16))
        x = _matmul_resid(hid, ffn_w_down[l].astype(BF16), x, g2)

    y_prompt, y_sample = split_streams(x)
    names = ("a_k", "a_v", "a_ik", "b_conv", "c_s", "d_k", "d_v", "d_lf")
    return ((y_prompt, y_sample) + tuple(jnp.stack(outs_p[n]) for n in names)
            + tuple(jnp.stack(outs_s[n]) for n in names))
```

```python
import functools
import math

import numpy as np
import jax
import jax.numpy as jnp
from jax import lax
from jax.experimental import pallas as pl
from jax.experimental.pallas import tpu as pltpu

F32 = jnp.float32
BF16 = jnp.bfloat16

D_MODEL = 4096
CHUNK = 64
Q_BLOCK = 128
HEAD_DIM = 128
ROPE_THETA = 500000.0
ROT_FRACTION = 4
NORM_EPS = 1e-6
A_HEADS = D_MODEL // 256
A_KV_HEADS = A_HEADS // 4
IDX_HEADS = D_MODEL // 128
IDX_DIM = 64
TOPK_MAX = 256
B_WIDTH = D_MODEL // 2
CONV_W = 3
C_HEADS = D_MODEL // 256
C_DK = 128
C_DV = 128
D_HEADS = D_MODEL // 256
AB_SIZES = (A_HEADS * HEAD_DIM, A_KV_HEADS * HEAD_DIM, A_KV_HEADS * HEAD_DIM,
            IDX_HEADS * IDX_DIM, IDX_DIM, IDX_HEADS, B_WIDTH, B_WIDTH, B_WIDTH)
CD_SIZES = (C_HEADS * C_DK, C_HEADS * C_DK, C_HEADS * C_DV, C_HEADS * C_DV,
            D_HEADS * HEAD_DIM, D_HEADS * HEAD_DIM, D_HEADS * HEAD_DIM, D_HEADS, D_HEADS * HEAD_DIM)

LANES = 128
SUBLANES = 8
VMEM_LIMIT_BYTES = 56 * 1024 * 1024

ROW_GROUP = 16
N_ALIGN = 512
INT_MIN = -2 ** 31


def _cparams(sem):
    return pltpu.CompilerParams(dimension_semantics=sem, vmem_limit_bytes=VMEM_LIMIT_BYTES)


def _pick_tile(n, candidates):
    for c in candidates:
        if n % c == 0:
            return c
    raise ValueError(f"no tile for {n}")


def _ada_kernel(c_ref, w_ref, b_ref, o_ref):
    c = c_ref[...]
    a = (c * jax.nn.sigmoid(c)).astype(BF16)
    o_ref[...] = jnp.dot(a, w_ref[...].astype(BF16), preferred_element_type=F32) + b_ref[...]


def _ada_mod(c, w, b):
    r, d = c.shape
    n = w.shape[1]
    tn = _pick_tile(n, (512, 256, 128))
    return pl.pallas_call(
        _ada_kernel,
        grid=(n // tn,),
        in_specs=[pl.BlockSpec((r, d), lambda j: (0, 0)),
                  pl.BlockSpec((d, tn), lambda j: (0, j)),
                  pl.BlockSpec((1, tn), lambda j: (0, j))],
        out_specs=pl.BlockSpec((r, tn), lambda j: (0, j)),
        out_shape=jax.ShapeDtypeStruct((r, n), F32),
        compiler_params=_cparams(("parallel",)),
        name="ada_mod",
    )(c, w, b)


def _norm_mod_kernel(x_ref, gain_ref, sc_ref, sh_ref, o_ref, *, groups):
    gain = gain_ref[...]
    for g in range(groups):
        rows = slice(g * ROW_GROUP, (g + 1) * ROW_GROUP)
        x = x_ref[rows, :]
        ms = jnp.mean(x * x, axis=-1, keepdims=True)
        y = x * lax.rsqrt(ms + NORM_EPS) * gain
        o_ref[rows, :] = (y * (1.0 + sc_ref[g:g + 1, :]) + sh_ref[g:g + 1, :]).astype(o_ref.dtype)


def _norm_mod(x, gain, sc, sh):
    m, d = x.shape
    tm = _pick_tile(m, (384, 256, 128))
    groups = tm // ROW_GROUP
    return pl.pallas_call(
        functools.partial(_norm_mod_kernel, groups=groups),
        grid=(m // tm,),
        in_specs=[pl.BlockSpec((tm, d), lambda i: (i, 0)),
                  pl.BlockSpec((1, d), lambda i: (0, 0)),
                  pl.BlockSpec((groups, d), lambda i: (i, 0)),
                  pl.BlockSpec((groups, d), lambda i: (i, 0))],
        out_specs=pl.BlockSpec((tm, d), lambda i: (i, 0)),
        out_shape=jax.ShapeDtypeStruct((m, d), BF16),
        compiler_params=_cparams(("parallel",)),
        name="norm_mod",
    )(x, gain, sc, sh)


def _mm_kernel(a_ref, w_ref, o_ref):
    o_ref[...] = jnp.dot(a_ref[...], w_ref[...], preferred_element_type=F32).astype(o_ref.dtype)


def _matmul(a, w, out_dtype=F32):
    m, k = a.shape
    n = w.shape[1]
    tm = _pick_tile(m, (768, 512, 256))
    tn = _pick_tile(n, (512, 256, 128))
    return pl.pallas_call(
        _mm_kernel,
        grid=(m // tm, n // tn),
        in_specs=[pl.BlockSpec((tm, k), lambda i, j: (i, 0)),
                  pl.BlockSpec((k, tn), lambda i, j: (0, j))],
        out_specs=pl.BlockSpec((tm, tn), lambda i, j: (i, j)),
        out_shape=jax.ShapeDtypeStruct((m, n), out_dtype),
        compiler_params=_cparams(("parallel", "parallel")),
        name="proj_in",
    )(a, w)


def _mm_resid_kernel(a_ref, w_ref, x_ref, g_ref, o_ref, acc_ref, *, nk, groups):
    k = pl.program_id(2)
    part = jnp.dot(a_ref[...], w_ref[...], preferred_element_type=F32)

    def finish(acc):
        for g in range(groups):
            rows = slice(g * ROW_GROUP, (g + 1) * ROW_GROUP)
            o_ref[rows, :] = x_ref[rows, :] + g_ref[g:g + 1, :] * acc[rows, :]

    if nk == 1:
        finish(part)
    else:
        @pl.when(k == 0)
        def _():
            acc_ref[...] = part

        @pl.when(jnp.logical_and(k > 0, k < nk - 1))
        def _():
            acc_ref[...] += part

        @pl.when(k == nk - 1)
        def _():
            finish(acc_ref[...] + part)


def _matmul_resid(a, w, x, gate):
    m, k = a.shape
    n = w.shape[1]
    tm = _pick_tile(m, (768, 512, 256))
    tn = _pick_tile(n, (512, 256, 128))
    tk = k if k <= 4096 else _pick_tile(k, (5504, 4096, 2048, 1024))
    nk = k // tk
    groups = tm // ROW_GROUP
    return pl.pallas_call(
        functools.partial(_mm_resid_kernel, nk=nk, groups=groups),
        grid=(m // tm, n // tn, nk),
        in_specs=[pl.BlockSpec((tm, tk), lambda i, j, kk: (i, kk)),
                  pl.BlockSpec((tk, tn), lambda i, j, kk: (kk, j)),
                  pl.BlockSpec((tm, tn), lambda i, j, kk: (i, j)),
                  pl.BlockSpec((groups, tn), lambda i, j, kk: (i, j))],
        out_specs=pl.BlockSpec((tm, tn), lambda i, j, kk: (i, j)),
        out_shape=jax.ShapeDtypeStruct((m, n), F32),
        scratch_shapes=[pltpu.VMEM((tm, tn), F32)],
        compiler_params=_cparams(("parallel", "parallel", "arbitrary")),
        name="proj_out_resid",
    )(a, w, x, gate)


def _ffn_gate_up_kernel(a_ref, wg_ref, wu_ref, o_ref):
    a = a_ref[...]
    g = jnp.dot(a, wg_ref[...], preferred_element_type=F32)
    u = jnp.dot(a, wu_ref[...], preferred_element_type=F32)
    o_ref[...] = (g * jax.nn.sigmoid(g) * u).astype(o_ref.dtype)


def _ffn_gate_up(a, wg, wu):
    m, k = a.shape
    n = wg.shape[1]
    tm = _pick_tile(m, (768, 512, 256))
    tn = _pick_tile(n, (512, 256, 128))
    return pl.pallas_call(
        _ffn_gate_up_kernel,
        grid=(m // tm, n // tn),
        in_specs=[pl.BlockSpec((tm, k), lambda i, j: (i, 0)),
                  pl.BlockSpec((k, tn), lambda i, j: (0, j)),
                  pl.BlockSpec((k, tn), lambda i, j: (0, j))],
        out_specs=pl.BlockSpec((tm, tn), lambda i, j: (i, j)),
        out_shape=jax.ShapeDtypeStruct((m, n), BF16),
        compiler_params=_cparams(("parallel", "parallel")),
        name="ffn_gate_up",
    )(a, wg, wu)


def _rms(x, g):
    xf = x.astype(F32)
    y = xf * lax.rsqrt(jnp.mean(xf * xf, axis=-1, keepdims=True) + NORM_EPS)
    return y * g.astype(F32)


def _rope(x, pos):
    d = x.shape[-1]
    r = d // ROT_FRACTION
    half = r // 2
    inv = 1.0 / (ROPE_THETA ** (jnp.arange(half, dtype=F32) / half))
    ang = pos.astype(F32)[:, None] * inv[None, :]
    cos = jnp.cos(ang)[:, None, :]
    sin = jnp.sin(ang)[:, None, :]
    x1 = x[..., :half]
    x2 = x[..., half:r]
    return jnp.concatenate([x1 * cos - x2 * sin, x2 * cos + x1 * sin, x[..., r:]], axis=-1)


def _dsa_kernel(qi_ref, kilo_ref, kihi_ref, wi_ref, q_ref, k_ref, v_ref, o_ref, key_scr, bias_scr, *,
                q_pos0, l_valid, n_sel):
    tq = q_ref.shape[0]
    lp = k_ref.shape[0]
    dn = (((1,), (1,)), ((), ()))

    kilo = kilo_ref[...]
    kihi = kihi_ref[...]
    wi = wi_ref[...] * ((IDX_HEADS * IDX_DIM) ** -0.5)
    score = jnp.zeros((tq, lp), F32)
    for j in range(IDX_HEADS // 2):
        qi2 = qi_ref[:, j * LANES:(j + 1) * LANES].astype(BF16)
        s_lo = lax.dot_general(qi2, kilo, dn, preferred_element_type=F32)
        s_hi = lax.dot_general(qi2, kihi, dn, preferred_element_type=F32)
        score = score + jnp.maximum(s_lo, 0.0) * wi[:, 2 * j:2 * j + 1]
        score = score + jnp.maximum(s_hi, 0.0) * wi[:, 2 * j + 1:2 * j + 2]

    q_pos = q_pos0 + pl.program_id(1) * tq + lax.broadcasted_iota(jnp.int32, (tq, lp), 0)
    k_pos = lax.broadcasted_iota(jnp.int32, (tq, lp), 1)
    shift = CHUNK.bit_length() - 1
    adm = jnp.logical_and(jnp.right_shift(k_pos, shift) <= jnp.right_shift(q_pos, shift), k_pos < l_valid)
    bits = pltpu.bitcast(score, jnp.int32)
    key = jnp.where(bits < 0, jnp.bitwise_xor(bits, jnp.int32(0x7FFFFFFF)), bits)
    key_scr[...] = jnp.where(adm, key, jnp.int32(INT_MIN))

    def count_ge(t):
        return jnp.sum(jnp.where(key_scr[...] >= t, 1.0, 0.0), axis=-1, keepdims=True)

    zero = jnp.zeros((tq, 1), jnp.int32)
    t0 = jnp.where(count_ge(zero) >= n_sel, zero, jnp.int32(INT_MIN))

    def body(i, t):
        cand = jnp.bitwise_or(t, jnp.left_shift(jnp.int32(1), 30 - i))
        return jnp.where(count_ge(cand) >= n_sel, cand, t)

    thr = lax.fori_loop(0, 31, body, t0)
    sel = jnp.logical_and(key_scr[...] >= thr, adm)
    bias_scr[...] = jnp.where(sel, 0.0, -jnp.inf)

    group = A_HEADS // A_KV_HEADS
    scale = HEAD_DIM ** -0.5
    for kvh in range(A_KV_HEADS):
        kk = k_ref[:, kvh * HEAD_DIM:(kvh + 1) * HEAD_DIM]
        vv = v_ref[:, kvh * HEAD_DIM:(kvh + 1) * HEAD_DIM]
        for g in range(group):
            cols = slice((kvh * group + g) * HEAD_DIM, (kvh * group + g + 1) * HEAD_DIM)
            qh = (q_ref[:, cols] * scale).astype(BF16)
            s = lax.dot_general(qh, kk, dn, preferred_element_type=F32) + bias_scr[...]
            m = jnp.max(s, axis=-1, keepdims=True)
            p = jnp.exp(s - m)
            l = jnp.sum(p, axis=-1, keepdims=True)
            o_ref[:, cols] = jnp.dot(p.astype(BF16), vv, preferred_element_type=F32) / l


def dsa_attention(qi, ki_lo, ki_hi, wi, q, k, v, *, tq, q_pos0, l_valid, n_sel):
    b, t, _ = q.shape
    lp = k.shape[1]
    qmap = lambda bi, i: (bi, i, 0)
    kmap = lambda bi, i: (bi, 0, 0)
    return pl.pallas_call(
        functools.partial(_dsa_kernel, q_pos0=q_pos0, l_valid=l_valid, n_sel=n_sel),
        grid=(b, t // tq),
        in_specs=[pl.BlockSpec((None, tq, qi.shape[2]), qmap),
                  pl.BlockSpec((None, lp, LANES), kmap),
                  pl.BlockSpec((None, lp, LANES), kmap),
                  pl.BlockSpec((None, tq, wi.shape[2]), qmap),
                  pl.BlockSpec((None, tq, q.shape[2]), qmap),
                  pl.BlockSpec((None, lp, k.shape[2]), kmap),
                  pl.BlockSpec((None, lp, v.shape[2]), kmap)],
        out_specs=pl.BlockSpec((None, tq, q.shape[2]), qmap),
        out_shape=jax.ShapeDtypeStruct(q.shape, F32),
        scratch_shapes=[pltpu.VMEM((tq, lp), jnp.int32), pltpu.VMEM((tq, lp), F32)],
        compiler_params=_cparams(("parallel", "arbitrary")),
        name="dsa_attention",
    )(qi, ki_lo, ki_hi, wi, q, k, v)


def _pack_ki(ki, lp):
    b, l, d = ki.shape
    z = jnp.zeros((b, l, d), ki.dtype)
    pad = ((0, 0), (0, lp - l), (0, 0))
    lo = jnp.pad(jnp.concatenate([ki, z], -1), pad).astype(BF16)
    hi = jnp.pad(jnp.concatenate([z, ki], -1), pad).astype(BF16)
    return lo, hi


def _dsa_attention(q, k, v, qi, ki, wi, k_cache, v_cache, ki_cache, past_len):
    bn, t = q.shape[:2]
    flat = lambda a: a.reshape(a.shape[0], a.shape[1], -1)
    kf, vf, kif = flat(k), flat(v), ki
    if past_len:
        kf = jnp.concatenate([flat(k_cache), kf], axis=1)
        vf = jnp.concatenate([flat(v_cache), vf], axis=1)
        kif = jnp.concatenate([ki_cache, kif], axis=1)
    l = past_len + t
    lp = -(-l // LANES) * LANES
    pad = ((0, 0), (0, lp - l), (0, 0))
    ki_lo, ki_hi = _pack_ki(kif, lp)
    return dsa_attention(flat(qi), ki_lo, ki_hi, wi, flat(q),
                         jnp.pad(kf, pad).astype(BF16), jnp.pad(vf, pad).astype(BF16),
                         tq=min(Q_BLOCK, t), q_pos0=past_len, l_valid=l, n_sel=min(TOPK_MAX, l // 4))


def _short_conv(b_gate, c_gate, u_in, conv_w, conv_state):
    T = u_in.shape[1]
    u = c_gate * u_in
    full = jnp.concatenate([conv_state, u], axis=1)
    y = full[:, 0:T] * conv_w[0]
    for j in range(1, CONV_W):
        y = y + full[:, j:j + T] * conv_w[j]
    return b_gate * y, full[:, T:]


def _hgrn2(q, f_raw, i_in, g, lb, o_gain, s0):
    Bn, T, _ = q.shape

    def heads(a, d):
        return a.reshape(Bn, T, C_HEADS, d).transpose(0, 2, 1, 3)

    lbh = lb.reshape(C_HEADS, 1, C_DK)
    f = lbh + (1.0 - lbh) * jax.nn.sigmoid(heads(f_raw, C_DK))
    logf = jnp.log(f)
    k = 1.0 - f
    qh = jax.nn.silu(heads(q, C_DK))
    vh = heads(i_in, C_DV)
    tc = min(CHUNK, T)
    nc = T // tc

    def to_chunks(a):
        return jnp.moveaxis(a.reshape(Bn, C_HEADS, nc, tc, a.shape[-1]), 2, 0)

    tri = jnp.tril(jnp.ones((tc, tc), dtype=bool))[:, :, None]

    def step(S, xs):
        qc, lc, kc, vc = xs
        A = jnp.cumsum(lc, axis=2)
        o_inter = jnp.einsum('bhtd,bhdv->bhtv', qc * jnp.exp(A), S)
        expo = A[:, :, :, None, :] - A[:, :, None, :, :]
        dec = jnp.where(tri, jnp.exp(jnp.where(tri, expo, 0.0)), 0.0)
        sc = jnp.einsum('bhtd,bhtsd,bhsd->bhts', qc, dec, kc)
        o = o_inter + jnp.einsum('bhts,bhsv->bhtv', sc, vc)
        A_last = A[:, :, -1:, :]
        S_new = (jnp.exp(A_last[:, :, 0, :])[..., None] * S
                 + jnp.einsum('bhsd,bhsv->bhdv', kc * jnp.exp(A_last - A), vc))
        return S_new, o

    s_fin, o = lax.scan(step, s0, (to_chunks(qh), to_chunks(logf), to_chunks(k), to_chunks(vh)))
    o = o.transpose(1, 2, 0, 3, 4).reshape(Bn, C_HEADS, T, C_DV).transpose(0, 2, 1, 3)
    o = _rms(o, o_gain) * jax.nn.silu(g.reshape(Bn, T, C_HEADS, C_DV))
    return o.reshape(Bn, T, C_HEADS * C_DV), s_fin


def _fox_kernel(q_ref, k_ref, v_ref, cq_ref, ck_ref, g_ref, o_ref, m_scr, l_scr, acc_scr, *, scale):
    qi = pl.program_id(2)
    ki = pl.program_id(3)
    tq, tk = q_ref.shape[0], k_ref.shape[0]

    @pl.when(ki == 0)
    def _():
        m_scr[...] = jnp.full(m_scr.shape, -jnp.inf, F32)
        l_scr[...] = jnp.zeros(l_scr.shape, F32)
        acc_scr[...] = jnp.zeros(acc_scr.shape, F32)

    def step(masked):
        q = (q_ref[...] * scale).astype(BF16)
        k = k_ref[...].astype(BF16)
        s = lax.dot_general(q, k, (((1,), (1,)), ((), ())), preferred_element_type=F32)
        s = s + cq_ref[...] - ck_ref[...]
        if masked:
            row = lax.broadcasted_iota(jnp.int32, (tq, tk), 0)
            col = lax.broadcasted_iota(jnp.int32, (tq, tk), 1)
            s = jnp.where(col <= row, s, -jnp.inf)
        m_prev = m_scr[...]
        m_new = jnp.maximum(m_prev, jnp.max(s, axis=-1, keepdims=True))
        alpha = jnp.exp(m_prev - m_new)
        p = jnp.exp(s - m_new)
        l_scr[...] = alpha * l_scr[...] + jnp.sum(p, axis=-1, keepdims=True)
        acc_scr[...] = alpha * acc_scr[...] + jnp.dot(p.astype(BF16), v_ref[...].astype(BF16),
                                                      preferred_element_type=F32)
        m_scr[...] = m_new

    @pl.when(ki < qi)
    def _():
        step(False)

    @pl.when(ki == qi)
    def _():
        step(True)
        g = g_ref[...]
        o_ref[...] = acc_scr[...] / l_scr[...] * jax.nn.sigmoid(g)


def fox_prefill(q, k, v, g, cum_col, cum_row, n_heads, tq=512):
    b, t, _ = q.shape
    tq = min(tq, t)
    nq = t // tq
    grid = (b, n_heads, nq, nq)
    qmap = lambda bi, h, qi, ki: (bi, qi, h)
    kmap = lambda bi, h, qi, ki: (bi, jnp.minimum(ki, qi), h)
    return pl.pallas_call(
        functools.partial(_fox_kernel, scale=HEAD_DIM ** -0.5),
        grid=grid,
        in_specs=[pl.BlockSpec((None, tq, HEAD_DIM), qmap),
                  pl.BlockSpec((None, tq, HEAD_DIM), kmap),
                  pl.BlockSpec((None, tq, HEAD_DIM), kmap),
                  pl.BlockSpec((None, None, tq, 1), lambda bi, h, qi, ki: (bi, h, qi, 0)),
                  pl.BlockSpec((None, None, 1, tq), lambda bi, h, qi, ki: (bi, h, 0, jnp.minimum(ki, qi))),
                  pl.BlockSpec((None, tq, HEAD_DIM), qmap)],
        out_specs=pl.BlockSpec((None, tq, HEAD_DIM), qmap),
        out_shape=jax.ShapeDtypeStruct(q.shape, F32),
        scratch_shapes=[pltpu.VMEM((tq, 1), F32), pltpu.VMEM((tq, 1), F32), pltpu.VMEM((tq, HEAD_DIM), F32)],
        compiler_params=_cparams(("parallel", "parallel", "parallel", "arbitrary")),
        name="fox_prefill",
    )(q, k, v, cum_col, cum_row, g)


def _fox_dec_kernel(q_ref, kc_ref, vc_ref, kn_ref, vn_ref, cq_ref, ckc_ref, ckn_ref, g_ref, o_ref, *, scale):
    t = q_ref.shape[0]
    q = (q_ref[...] * scale).astype(BF16)
    dn = (((1,), (1,)), ((), ()))
    s_c = lax.dot_general(q, kc_ref[...].astype(BF16), dn, preferred_element_type=F32)
    s_c = s_c + cq_ref[...] - ckc_ref[...]
    s_n = lax.dot_general(q, kn_ref[...].astype(BF16), dn, preferred_element_type=F32)
    s_n = s_n + cq_ref[...] - ckn_ref[...]
    row = lax.broadcasted_iota(jnp.int32, (t, t), 0)
    col = lax.broadcasted_iota(jnp.int32, (t, t), 1)
    s_n = jnp.where(col <= row, s_n, -jnp.inf)
    m = jnp.maximum(jnp.max(s_c, axis=-1, keepdims=True), jnp.max(s_n, axis=-1, keepdims=True))
    p_c = jnp.exp(s_c - m)
    p_n = jnp.exp(s_n - m)
    l = jnp.sum(p_c, axis=-1, keepdims=True) + jnp.sum(p_n, axis=-1, keepdims=True)
    acc = jnp.dot(p_c.astype(BF16), vc_ref[...].astype(BF16), preferred_element_type=F32)
    acc = acc + jnp.dot(p_n.astype(BF16), vn_ref[...].astype(BF16), preferred_element_type=F32)
    o_ref[...] = acc / l * jax.nn.sigmoid(g_ref[...])


def fox_decode(q, k_cache, v_cache, k_new, v_new, g, cq_col, ck_cache_row, ck_new_row, n_heads):
    b, t, _ = q.shape
    p = k_cache.shape[1]
    new = lambda bi, h: (bi, 0, h)
    return pl.pallas_call(
        functools.partial(_fox_dec_kernel, scale=HEAD_DIM ** -0.5),
        grid=(b, n_heads),
        in_specs=[pl.BlockSpec((None, t, HEAD_DIM), new),
                  pl.BlockSpec((None, p, HEAD_DIM), new),
                  pl.BlockSpec((None, p, HEAD_DIM), new),
                  pl.BlockSpec((None, t, HEAD_DIM), new),
                  pl.BlockSpec((None, t, HEAD_DIM), new),
                  pl.BlockSpec((None, None, t, 1), lambda bi, h: (bi, h, 0, 0)),
                  pl.BlockSpec((None, None, 1, p), lambda bi, h: (bi, h, 0, 0)),
                  pl.BlockSpec((None, None, 1, t), lambda bi, h: (bi, h, 0, 0)),
                  pl.BlockSpec((None, t, HEAD_DIM), new)],
        out_specs=pl.BlockSpec((None, t, HEAD_DIM), new),
        out_shape=jax.ShapeDtypeStruct(q.shape, F32),
        compiler_params=_cparams(("parallel", "parallel")),
        name="fox_decode",
    )(q, k_cache, v_cache, k_new, v_new, cq_col, ck_cache_row, ck_new_row, g)


def _fox_attention(q, k, v, f_logit, g, k_cache, v_cache, lf_cache, past_len):
    flat = lambda a: a.reshape(a.shape[0], a.shape[1], -1)
    logf = jax.nn.log_sigmoid(f_logit)
    cum = jnp.cumsum(jnp.concatenate([lf_cache, logf], axis=1), axis=1).transpose(0, 2, 1)
    if past_len:
        o = fox_decode(flat(q), flat(k_cache), flat(v_cache), flat(k), flat(v), g,
                       cum[:, :, past_len:, None], cum[:, :, None, :past_len], cum[:, :, None, past_len:], D_HEADS)
    else:
        o = fox_prefill(flat(q), flat(k), flat(v), g, cum[..., None], cum[:, :, None, :], D_HEADS)
    return o, logf


def _pad_cols(w, n):
    return jnp.pad(w, ((0, 0), (0, n - w.shape[1])))


def _ab_in_weight(w):
    offs = np.cumsum((0,) + AB_SIZES)
    main = jnp.concatenate([w[:, :offs[4]], w[:, offs[6]:]], axis=1)
    tail = _pad_cols(w[:, offs[4]:offs[6]], N_ALIGN)
    return jnp.concatenate([main, tail], axis=1).astype(BF16)


def _cd_in_weight(w):
    offs = np.cumsum((0,) + CD_SIZES)
    main = jnp.concatenate([w[:, :offs[7]], w[:, offs[8]:]], axis=1)
    tail = _pad_cols(w[:, offs[7]:offs[8]], N_ALIGN)
    return jnp.concatenate([main, tail], axis=1).astype(BF16)


def _expand_groups(mod_p, mod_s, seq_p):
    return jnp.concatenate([jnp.repeat(mod_p, seq_p // ROW_GROUP, axis=0), mod_s], axis=0)


def kernel(x_prompt, x_sample, c_prompt, c_sample, cache_a_k, cache_a_v, cache_a_idx_k, state_b_conv, state_c_s, cache_d_k, cache_d_v, cache_d_logf, ada_w, ada_b, norm_mix, norm_ffn, ab_w_in, ab_w_out, a_q_norm, a_k_norm, b_conv_w, cd_w_in, cd_w_out, c_lb, c_o_norm, d_q_norm, d_k_norm, d_f_bias, ffn_w_gate, ffn_w_up, ffn_w_down):
    bp, tp, d = x_prompt.shape
    bs, ts, _ = x_sample.shape
    past = cache_a_k.shape[2]
    depth = ada_w.shape[0]
    assert ts == ROW_GROUP and tp % ROW_GROUP == 0 and d == D_MODEL
    mp, ms = bp * tp, bs * ts

    x = jnp.concatenate([x_prompt.reshape(mp, d), x_sample.reshape(ms, d)], axis=0)
    c_all = jnp.concatenate([c_prompt, c_sample], axis=0)
    c_rows = -(-c_all.shape[0] // SUBLANES) * SUBLANES
    c_pad = jnp.pad(c_all, ((0, c_rows - c_all.shape[0]), (0, 0)))

    lb_all = jnp.cumsum(jax.nn.softmax(c_lb.astype(F32), axis=0), axis=0)
    lb_all = lb_all - lb_all[0]

    def split_streams(z):
        return z[:mp].reshape(bp, tp, -1), z[mp:].reshape(bs, ts, -1)

    outs_p, outs_s = {}, {}
    for l in range(depth):
        mod = _ada_mod(c_pad, ada_w[l], ada_b[l][None, :])
        mods = [_expand_groups(m[:bp], m[bp:bp + bs], tp) for m in jnp.split(mod, 6, axis=-1)]
        sh1, sc1, g1, sh2, sc2, g2 = mods
        h = _norm_mod(x, norm_mix[l][None, :], sc1, sh1)
        if l % 2 == 0:
            e = l // 2
            z = _matmul(h, _ab_in_weight(ab_w_in[e]))
            n_q, n_kv, n_qi = AB_SIZES[0], AB_SIZES[1], AB_SIZES[3]
            o = 0
            zq = z[:, o:o + n_q]; o += n_q
            zk = z[:, o:o + n_kv]; o += n_kv
            zv = z[:, o:o + n_kv]; o += n_kv
            zqi = z[:, o:o + n_qi]; o += n_qi
            zbg = z[:, o:o + B_WIDTH]; o += B_WIDTH
            zcg = z[:, o:o + B_WIDTH]; o += B_WIDTH
            zbin = z[:, o:o + B_WIDTH]; o += B_WIDTH
            zki = z[:, o:o + IDX_DIM]
            zwi = z[:, o + IDX_DIM:o + IDX_DIM + IDX_HEADS]
            ys = []
            for (sl, bn, t, pl_, kc, vc, ikc, cs, od) in (
                    (slice(0, mp), bp, tp, 0, None, None, None, None, outs_p),
                    (slice(mp, mp + ms), bs, ts, past, cache_a_k[e], cache_a_v[e], cache_a_idx_k[e],
                     state_b_conv[e], outs_s)):
                pos = pl_ + jnp.arange(t)
                if kc is None:
                    kc = jnp.zeros((bn, 0, A_KV_HEADS, HEAD_DIM), F32)
                    vc = jnp.zeros((bn, 0, A_KV_HEADS, HEAD_DIM), F32)
                    ikc = jnp.zeros((bn, 0, IDX_DIM), F32)
                    cs = jnp.zeros((bn, CONV_W - 1, B_WIDTH), F32)
                q = _rope(_rms(zq[sl].reshape(bn, t, A_HEADS, HEAD_DIM), a_q_norm[e]), pos)
                k = _rope(_rms(zk[sl].reshape(bn, t, A_KV_HEADS, HEAD_DIM), a_k_norm[e]), pos)
                v = zv[sl].reshape(bn, t, A_KV_HEADS, HEAD_DIM)
                qi = _rope(zqi[sl].reshape(bn, t, IDX_HEADS, IDX_DIM), pos)
                ki = _rope(zki[sl].reshape(bn, t, 1, IDX_DIM), pos)[:, :, 0, :]
                wi = zwi[sl].reshape(bn, t, IDX_HEADS)
                ya = _dsa_attention(q, k, v, qi, ki, wi, kc, vc, ikc, pl_)
                yb, conv_new = _short_conv(zbg[sl].reshape(bn, t, B_WIDTH), zcg[sl].reshape(bn, t, B_WIDTH),
                                           zbin[sl].reshape(bn, t, B_WIDTH), b_conv_w[e], cs)
                ys.append(jnp.concatenate([ya, yb], axis=-1).reshape(bn * t, -1))
                od.setdefault("a_k", []).append(k)
                od.setdefault("a_v", []).append(v)
                od.setdefault("a_ik", []).append(ki)
                od.setdefault("b_conv", []).append(conv_new)
            y = jnp.concatenate(ys, axis=0).astype(BF16)
            x = _matmul_resid(y, ab_w_out[e].astype(BF16), x, g1)
        else:
            od_ = l // 2
            z = _matmul(h, _cd_in_weight(cd_w_in[od_]))
            w = C_HEADS * C_DK
            zqc, zfc, zic, zgc, zqd, zkd, zvd, zgd = [z[:, i * w:(i + 1) * w] for i in range(8)]
            zfd = z[:, 8 * w:8 * w + D_HEADS]
            ys = []
            for (sl, bn, t, pl_, s0, kc, vc, lfc, od) in (
                    (slice(0, mp), bp, tp, 0, None, None, None, None, outs_p),
                    (slice(mp, mp + ms), bs, ts, past, state_c_s[od_], cache_d_k[od_], cache_d_v[od_],
                     cache_d_logf[od_], outs_s)):
                if s0 is None:
                    s0 = jnp.zeros((bn, C_HEADS, C_DK, C_DV), F32)
                    kc = jnp.zeros((bn, 0, D_HEADS, HEAD_DIM), F32)
                    vc = jnp.zeros((bn, 0, D_HEADS, HEAD_DIM), F32)
                    lfc = jnp.zeros((bn, 0, D_HEADS), F32)
                r3 = lambda a: a[sl].reshape(bn, t, -1)
                yc, s_new = _hgrn2(r3(zqc), r3(zfc), r3(zic), r3(zgc), lb_all[l], c_o_norm[od_], s0)
                qd = _rms(r3(zqd).reshape(bn, t, D_HEADS, HEAD_DIM), d_q_norm[od_])
                kd = _rms(r3(zkd).reshape(bn, t, D_HEADS, HEAD_DIM), d_k_norm[od_])
                vd = r3(zvd).reshape(bn, t, D_HEADS, HEAD_DIM)
                yd, lf_new = _fox_attention(qd, kd, vd, r3(zfd) + d_f_bias[od_], r3(zgd), kc, vc, lfc, pl_)
                ys.append(jnp.concatenate([yc, yd], axis=-1).reshape(bn * t, -1))
                od.setdefault("c_s", []).append(s_new)
                od.setdefault("d_k", []).append(kd)
                od.setdefault("d_v", []).append(vd)
                od.setdefault("d_lf", []).append(lf_new)
            y = jnp.concatenate(ys, axis=0).astype(BF16)
            x = _matmul_resid(y, cd_w_out[od_].astype(BF16), x, g1)
        h2 = _norm_mod(x, norm_ffn[l][None, :], sc2, sh2)
        hid = _ffn_gate_up(h2, ffn_w_gate[l].astype(BF16), ffn_w_up[l].astype(BF16))
        x = _matmul_resid(hid, ffn_w_down[l].astype(BF16), x, g2)

    y_prompt, y_sample = split_streams(x)
    names = ("a_k", "a_v", "a_ik", "b_conv", "c_s", "d_k", "d_v", "d_lf")
    return ((y_prompt, y_sample) + tuple(jnp.stack(outs_p[n]) for n in names)
            + tuple(jnp.stack(outs_s[n]) for n in names))
```

```python
import functools

import numpy as np
import jax
import jax.numpy as jnp
from jax import lax
from jax.experimental import pallas as pl
from jax.experimental.pallas import tpu as pltpu

F32 = jnp.float32
BF16 = jnp.bfloat16

D_MODEL = 4096
CHUNK = 64
Q_BLOCK = 128
HEAD_DIM = 128
ROPE_THETA = 500000.0
ROT_FRACTION = 4
NORM_EPS = 1e-6
A_HEADS = D_MODEL // 256
A_KV_HEADS = A_HEADS // 4
IDX_HEADS = D_MODEL // 128
IDX_DIM = 64
TOPK_MAX = 256
B_WIDTH = D_MODEL // 2
CONV_W = 3
C_HEADS = D_MODEL // 256
C_DK = 128
C_DV = 128
D_HEADS = D_MODEL // 256
AB_SIZES = (A_HEADS * HEAD_DIM, A_KV_HEADS * HEAD_DIM, A_KV_HEADS * HEAD_DIM,
            IDX_HEADS * IDX_DIM, IDX_DIM, IDX_HEADS, B_WIDTH, B_WIDTH, B_WIDTH)
CD_SIZES = (C_HEADS * C_DK, C_HEADS * C_DK, C_HEADS * C_DV, C_HEADS * C_DV,
            D_HEADS * HEAD_DIM, D_HEADS * HEAD_DIM, D_HEADS * HEAD_DIM, D_HEADS, D_HEADS * HEAD_DIM)

LANES = 128
SUBLANES = 8
VMEM_LIMIT_BYTES = 56 * 1024 * 1024

ROW_GROUP = 16
N_ALIGN = 512
INT_MIN = -2 ** 31
HGRN_UNROLL = 4

AB_Q, AB_QI, AB_K, AB_V = 0, 2048, 4096, 4608
AB_BG, AB_CG, AB_BIN, AB_KW = 5120, 7168, 9216, 11264
CD_W = 2048
CD_FD = 8 * CD_W


def _cparams(sem):
    return pltpu.CompilerParams(dimension_semantics=sem, vmem_limit_bytes=VMEM_LIMIT_BYTES)


def _pick_tile(n, candidates):
    for c in candidates:
        if n % c == 0:
            return c
    raise ValueError(f"no tile for {n}")


M_TILES = (768, 512, 256, 128, 96, 64, 32, 16)


def _ada_kernel(c_ref, w_ref, b_ref, o_ref):
    c = c_ref[...]
    a = (c * jax.nn.sigmoid(c)).astype(BF16)
    o_ref[...] = jnp.dot(a, w_ref[...].astype(BF16), preferred_element_type=F32) + b_ref[...]


def _ada_mod(c, w, b):
    r, d = c.shape
    n = w.shape[1]
    tn = _pick_tile(n, (512, 256, 128))
    return pl.pallas_call(
        _ada_kernel,
        grid=(n // tn,),
        in_specs=[pl.BlockSpec((r, d), lambda j: (0, 0)),
                  pl.BlockSpec((d, tn), lambda j: (0, j)),
                  pl.BlockSpec((1, tn), lambda j: (0, j))],
        out_specs=pl.BlockSpec((r, tn), lambda j: (0, j)),
        out_shape=jax.ShapeDtypeStruct((r, n), F32),
        compiler_params=_cparams(("parallel",)),
        name="ada_mod",
    )(c, w, b)


def _norm_mod_kernel(x_ref, gain_ref, sc_ref, sh_ref, o_ref, *, groups):
    gain = gain_ref[...]
    for g in range(groups):
        rows = slice(g * ROW_GROUP, (g + 1) * ROW_GROUP)
        x = x_ref[rows, :]
        ms = jnp.mean(x * x, axis=-1, keepdims=True)
        y = x * lax.rsqrt(ms + NORM_EPS) * gain
        o_ref[rows, :] = (y * (1.0 + sc_ref[g:g + 1, :]) + sh_ref[g:g + 1, :]).astype(o_ref.dtype)


def _norm_mod(x, gain, sc, sh):
    m, d = x.shape
    tm = _pick_tile(m, (384,) + M_TILES)
    groups = tm // ROW_GROUP
    return pl.pallas_call(
        functools.partial(_norm_mod_kernel, groups=groups),
        grid=(m // tm,),
        in_specs=[pl.BlockSpec((tm, d), lambda i: (i, 0)),
                  pl.BlockSpec((1, d), lambda i: (0, 0)),
                  pl.BlockSpec((groups, d), lambda i: (i, 0)),
                  pl.BlockSpec((groups, d), lambda i: (i, 0))],
        out_specs=pl.BlockSpec((tm, d), lambda i: (i, 0)),
        out_shape=jax.ShapeDtypeStruct((m, d), BF16),
        compiler_params=_cparams(("parallel",)),
        name="norm_mod",
    )(x, gain, sc, sh)


def _mm_kernel(a_ref, w_ref, o_ref):
    o_ref[...] = jnp.dot(a_ref[...], w_ref[...], preferred_element_type=F32).astype(o_ref.dtype)


def _matmul(a, w):
    m, k = a.shape
    n = w.shape[1]
    tm = _pick_tile(m, M_TILES)
    tn = _pick_tile(n, (512, 256, 128))
    return pl.pallas_call(
        _mm_kernel,
        grid=(m // tm, n // tn),
        in_specs=[pl.BlockSpec((tm, k), lambda i, j: (i, 0)),
                  pl.BlockSpec((k, tn), lambda i, j: (0, j))],
        out_specs=pl.BlockSpec((tm, tn), lambda i, j: (i, j)),
        out_shape=jax.ShapeDtypeStruct((m, n), F32),
        compiler_params=_cparams(("parallel", "parallel")),
        name="proj_in",
    )(a, w)


def _gated_residual(o_ref, x_ref, g_ref, acc, groups):
    for g in range(groups):
        rows = slice(g * ROW_GROUP, (g + 1) * ROW_GROUP)
        o_ref[rows, :] = x_ref[rows, :] + g_ref[g:g + 1, :] * acc[rows, :]


def _mix_out_kernel(a1_ref, a2_ref, w1_ref, w2_ref, x_ref, g_ref, o_ref, *, groups):
    acc = jnp.dot(a1_ref[...], w1_ref[...].astype(BF16), preferred_element_type=F32)
    acc = acc + jnp.dot(a2_ref[...], w2_ref[...].astype(BF16), preferred_element_type=F32)
    _gated_residual(o_ref, x_ref, g_ref, acc, groups)


def _mix_out(a1, a2, w, x, gate):
    m, kh = a1.shape
    n = w.shape[1]
    assert w.shape[0] == 2 * kh
    tm = _pick_tile(m, M_TILES)
    tn = _pick_tile(n, (512, 256, 128))
    groups = tm // ROW_GROUP
    return pl.pallas_call(
        functools.partial(_mix_out_kernel, groups=groups),
        grid=(m // tm, n // tn),
        in_specs=[pl.BlockSpec((tm, kh), lambda i, j: (i, 0)),
                  pl.BlockSpec((tm, kh), lambda i, j: (i, 0)),
                  pl.BlockSpec((kh, tn), lambda i, j: (0, j)),
                  pl.BlockSpec((kh, tn), lambda i, j: (1, j)),
                  pl.BlockSpec((tm, tn), lambda i, j: (i, j)),
                  pl.BlockSpec((groups, tn), lambda i, j: (i, j))],
        out_specs=pl.BlockSpec((tm, tn), lambda i, j: (i, j)),
        out_shape=jax.ShapeDtypeStruct((m, n), F32),
        compiler_params=_cparams(("parallel", "parallel")),
        name="mix_out_resid",
    )(a1, a2, w, w, x, gate)


def _ffn_down_kernel(a_ref, w_ref, x_ref, g_ref, o_ref, acc_ref, *, nk, groups):
    k = pl.program_id(2)
    part = jnp.dot(a_ref[...], w_ref[...].astype(BF16), preferred_element_type=F32)
    if nk == 1:
        _gated_residual(o_ref, x_ref, g_ref, part, groups)
    else:
        @pl.when(k == 0)
        def _():
            acc_ref[...] = part

        @pl.when(jnp.logical_and(k > 0, k < nk - 1))
        def _():
            acc_ref[...] += part

        @pl.when(k == nk - 1)
        def _():
            _gated_residual(o_ref, x_ref, g_ref, acc_ref[...] + part, groups)


def _ffn_down(a, w, x, gate):
    m, k = a.shape
    n = w.shape[1]
    tm = _pick_tile(m, M_TILES)
    tn = _pick_tile(n, (256, 128))
    tk = k if k <= 4096 else _pick_tile(k, (5504, 4096, 2048, 1024))
    nk = k // tk
    groups = tm // ROW_GROUP
    return pl.pallas_call(
        functools.partial(_ffn_down_kernel, nk=nk, groups=groups),
        grid=(m // tm, n // tn, nk),
        in_specs=[pl.BlockSpec((tm, tk), lambda i, j, kk: (i, kk)),
                  pl.BlockSpec((tk, tn), lambda i, j, kk: (kk, j)),
                  pl.BlockSpec((tm, tn), lambda i, j, kk: (i, j)),
                  pl.BlockSpec((groups, tn), lambda i, j, kk: (i, j))],
        out_specs=pl.BlockSpec((tm, tn), lambda i, j, kk: (i, j)),
        out_shape=jax.ShapeDtypeStruct((m, n), F32),
        scratch_shapes=[pltpu.VMEM((tm, tn), F32)],
        compiler_params=_cparams(("parallel", "parallel", "arbitrary")),
        name="ffn_down_resid",
    )(a, w, x, gate)


def _ffn_gate_up_kernel(a_ref, wg_ref, wu_ref, o_ref):
    a = a_ref[...]
    g = jnp.dot(a, wg_ref[...].astype(BF16), preferred_element_type=F32)
    u = jnp.dot(a, wu_ref[...].astype(BF16), preferred_element_type=F32)
    o_ref[...] = (g * jax.nn.sigmoid(g) * u).astype(o_ref.dtype)


def _ffn_gate_up(a, wg, wu):
    m, k = a.shape
    n = wg.shape[1]
    tm = _pick_tile(m, (1056,) + M_TILES)
    tn = _pick_tile(n, (256, 128))
    return pl.pallas_call(
        _ffn_gate_up_kernel,
        grid=(m // tm, n // tn),
        in_specs=[pl.BlockSpec((tm, k), lambda i, j: (i, 0)),
                  pl.BlockSpec((k, tn), lambda i, j: (0, j)),
                  pl.BlockSpec((k, tn), lambda i, j: (0, j))],
        out_specs=pl.BlockSpec((tm, tn), lambda i, j: (i, j)),
        out_shape=jax.ShapeDtypeStruct((m, n), BF16),
        compiler_params=_cparams(("parallel", "parallel")),
        name="ffn_gate_up",
    )(a, wg, wu)


def _rope_lanes(y, cos, sin_signed, half, period):
    lane = lax.broadcasted_iota(jnp.int32, y.shape, 1)
    first = jnp.bitwise_and(lane, period - 1) < half
    swap = jnp.where(first, pltpu.roll(y, LANES - half, 1), pltpu.roll(y, half, 1))
    return y * cos + swap * sin_signed


def _head_rms(x, gain):
    ms = jnp.mean(x * x, axis=-1, keepdims=True)
    return x * lax.rsqrt(ms + NORM_EPS) * gain


def _prep_a_kernel(zq_ref, zqi_ref, zkv_ref, zkw_ref, cq_ref, sq_ref, ci_ref, si_ref, gq_ref, gk_ref,
                   q_ref, qi_ref, k_ref, kb_ref, vb_ref, ki_ref, kilo_ref, kihi_ref):
    cq, sq, ci, si = cq_ref[...], sq_ref[...], ci_ref[...], si_ref[...]
    half_q = HEAD_DIM // ROT_FRACTION // 2
    half_i = IDX_DIM // ROT_FRACTION // 2
    for h in range(A_HEADS):
        cols = slice(h * HEAD_DIM, (h + 1) * HEAD_DIM)
        y = _rope_lanes(_head_rms(zq_ref[:, cols], gq_ref[...]), cq, sq, half_q, HEAD_DIM)
        q_ref[:, cols] = (y * (HEAD_DIM ** -0.5)).astype(BF16)
    for h in range(A_KV_HEADS):
        cols = slice(h * HEAD_DIM, (h + 1) * HEAD_DIM)
        y = _rope_lanes(_head_rms(zkv_ref[:, cols], gk_ref[...]), cq, sq, half_q, HEAD_DIM)
        k_ref[:, cols] = y
        kb_ref[:, cols] = y.astype(BF16)
    nkv = A_KV_HEADS * HEAD_DIM
    vb_ref[...] = zkv_ref[:, nkv:2 * nkv].astype(BF16)
    for j in range(IDX_HEADS * IDX_DIM // LANES):
        cols = slice(j * LANES, (j + 1) * LANES)
        qi_ref[:, cols] = _rope_lanes(zqi_ref[:, cols], ci, si, half_i, IDX_DIM).astype(BF16)
    slab = _rope_lanes(zkw_ref[:, 0:LANES], ci, si, half_i, IDX_DIM)
    ki_ref[...] = slab[:, 0:IDX_DIM]
    lane = lax.broadcasted_iota(jnp.int32, slab.shape, 1)
    lo = jnp.where(lane < IDX_DIM, slab, 0.0)
    kilo_ref[...] = lo.astype(BF16)
    kihi_ref[...] = pltpu.roll(lo, IDX_DIM, 1).astype(BF16)


def _prep_a(z, cq, sq, ci, si, gq, gk):
    m = z.shape[0]
    tm = _pick_tile(m, (256, 128, 96, 64, 32, 16))
    row = lambda w, c: pl.BlockSpec((tm, w), lambda i: (i, c))
    full = lambda w: pl.BlockSpec((tm, w), lambda i: (i, 0))
    par = pl.BlockSpec((1, LANES), lambda i: (0, 0))
    nq, nkv = A_HEADS * HEAD_DIM, A_KV_HEADS * HEAD_DIM
    sds = lambda w, dt: jax.ShapeDtypeStruct((m, w), dt)
    return pl.pallas_call(
        _prep_a_kernel,
        grid=(m // tm,),
        in_specs=[row(nq, AB_Q // nq), row(nq, AB_QI // nq), row(2 * nkv, AB_K // (2 * nkv)),
                  row(N_ALIGN, AB_KW // N_ALIGN), full(LANES), full(LANES), full(LANES), full(LANES), par, par],
        out_specs=[full(nq), full(nq), full(nkv), full(nkv), full(nkv), full(IDX_DIM), full(LANES), full(LANES)],
        out_shape=[sds(nq, BF16), sds(nq, BF16), sds(nkv, F32), sds(nkv, BF16), sds(nkv, BF16),
                   sds(IDX_DIM, F32), sds(LANES, BF16), sds(LANES, BF16)],
        compiler_params=_cparams(("parallel",)),
        name="prep_a",
    )(z, z, z, z, cq, sq, ci, si, gq, gk)


def _rope_tables(pos, head_dim):
    r = head_dim // ROT_FRACTION
    half = r // 2
    inv = 1.0 / (ROPE_THETA ** (jnp.arange(half, dtype=F32) / half))
    ang = pos.astype(F32)[:, None] * inv[None, :]
    cos, sin = jnp.cos(ang), jnp.sin(ang)
    lane = np.arange(LANES) % head_dim
    idx = lane % half
    in_rot = jnp.asarray(lane < r)[None, :]
    first = jnp.asarray(lane < half)[None, :]
    c = jnp.where(in_rot, cos[:, idx], 1.0)
    s = jnp.where(first, -sin[:, idx], jnp.where(in_rot, sin[:, idx], 0.0))
    return c, s


def _dsa_kernel(qi_ref, kilo_ref, kihi_ref, wi_ref, q_ref, k_ref, v_ref, o_ref, key_scr, bias_scr, *,
                q_pos0, l_valid, n_sel):
    tq = q_ref.shape[0]
    lp = k_ref.shape[0]
    dn = (((1,), (1,)), ((), ()))

    kilo = kilo_ref[...]
    kihi = kihi_ref[...]
    wi = wi_ref[...] * ((IDX_HEADS * IDX_DIM) ** -0.5)
    score = jnp.zeros((tq, lp), F32)
    for j in range(IDX_HEADS // 2):
        qi2 = qi_ref[:, j * LANES:(j + 1) * LANES]
        s_lo = lax.dot_general(qi2, kilo, dn, preferred_element_type=F32)
        s_hi = lax.dot_general(qi2, kihi, dn, preferred_element_type=F32)
        c = IDX_DIM + 2 * j
        score = score + jnp.maximum(s_lo, 0.0) * wi[:, c:c + 1]
        score = score + jnp.maximum(s_hi, 0.0) * wi[:, c + 1:c + 2]

    q_pos = q_pos0 + pl.program_id(1) * tq + lax.broadcasted_iota(jnp.int32, (tq, lp), 0)
    k_pos = lax.broadcasted_iota(jnp.int32, (tq, lp), 1)
    shift = CHUNK.bit_length() - 1
    adm = jnp.logical_and(jnp.right_shift(k_pos, shift) <= jnp.right_shift(q_pos, shift), k_pos < l_valid)
    bits = pltpu.bitcast(score, jnp.int32)
    key = jnp.where(bits < 0, jnp.bitwise_xor(bits, jnp.int32(0x7FFFFFFF)), bits)
    key_scr[...] = jnp.where(adm, key, jnp.int32(INT_MIN))

    def count_ge(t):
        return jnp.sum(jnp.where(key_scr[...] >= t, 1.0, 0.0), axis=-1, keepdims=True)

    zero = jnp.zeros((tq, 1), jnp.int32)
    t0 = jnp.where(count_ge(zero) >= n_sel, zero, jnp.int32(INT_MIN))

    def body(i, t):
        cand = jnp.bitwise_or(t, jnp.left_shift(jnp.int32(1), 30 - i))
        return jnp.where(count_ge(cand) >= n_sel, cand, t)

    thr = lax.fori_loop(0, 31, body, t0)
    sel = jnp.logical_and(key_scr[...] >= thr, adm)
    bias_scr[...] = jnp.where(sel, 0.0, -jnp.inf)

    group = A_HEADS // A_KV_HEADS
    for kvh in range(A_KV_HEADS):
        kk = k_ref[:, kvh * HEAD_DIM:(kvh + 1) * HEAD_DIM]
        vv = v_ref[:, kvh * HEAD_DIM:(kvh + 1) * HEAD_DIM]
        for g in range(group):
            cols = slice((kvh * group + g) * HEAD_DIM, (kvh * group + g + 1) * HEAD_DIM)
            s = lax.dot_general(q_ref[:, cols], kk, dn, preferred_element_type=F32) + bias_scr[...]
            m = jnp.max(s, axis=-1, keepdims=True)
            p = jnp.exp(s - m)
            l = jnp.sum(p, axis=-1, keepdims=True)
            o = jnp.dot(p.astype(BF16), vv, preferred_element_type=F32) / l
            o_ref[:, cols] = o.astype(o_ref.dtype)


def _dsa_attention(qi, ki_lo, ki_hi, z, q, k, v, *, n_batch, t, tq, row0, lp, q_pos0, l_valid, n_sel):
    nq = t // tq
    qmap = lambda bi, i: (row0 // tq + bi * nq + i, 0)
    kmap = lambda bi, i: (bi, 0)
    return pl.pallas_call(
        functools.partial(_dsa_kernel, q_pos0=q_pos0, l_valid=l_valid, n_sel=n_sel),
        grid=(n_batch, nq),
        in_specs=[pl.BlockSpec((tq, qi.shape[1]), qmap),
                  pl.BlockSpec((lp, LANES), kmap),
                  pl.BlockSpec((lp, LANES), kmap),
                  pl.BlockSpec((tq, LANES), lambda bi, i: (row0 // tq + bi * nq + i, AB_KW // LANES)),
                  pl.BlockSpec((tq, q.shape[1]), qmap),
                  pl.BlockSpec((lp, k.shape[1]), kmap),
                  pl.BlockSpec((lp, v.shape[1]), kmap)],
        out_specs=pl.BlockSpec((tq, q.shape[1]), lambda bi, i: (bi * nq + i, 0)),
        out_shape=jax.ShapeDtypeStruct((n_batch * t, q.shape[1]), BF16),
        scratch_shapes=[pltpu.VMEM((tq, lp), jnp.int32), pltpu.VMEM((tq, lp), F32)],
        compiler_params=_cparams(("parallel", "arbitrary")),
        name="dsa_attention",
    )(qi, ki_lo, ki_hi, z, q, k, v)


def _conv_kernel(bg_ref, cg_ref, u_ref, st_ref, w_ref, y_ref, ns_ref):
    t = u_ref.shape[0]
    u = cg_ref[...] * u_ref[...]
    st = st_ref[...]
    row = lax.broadcasted_iota(jnp.int32, u.shape, 0)
    u1 = jnp.where(row == 0, st[1:2, :], pltpu.roll(u, 1, 0))
    u2 = jnp.where(row == 0, st[0:1, :], jnp.where(row == 1, st[1:2, :], pltpu.roll(u, 2, 0)))
    w = w_ref[...]
    y = u2 * w[0:1, :] + u1 * w[1:2, :] + u * w[2:3, :]
    y_ref[...] = (bg_ref[...] * y).astype(y_ref.dtype)
    ns_ref[...] = u[t - (CONV_W - 1):t, :]


def _short_conv(z, state, w, *, n_batch, t, row0):
    tc = 256
    nb = row0 // t
    col = lambda c0: pl.BlockSpec((t, tc), lambda bi, j: (nb + bi, c0 // tc + j))
    return pl.pallas_call(
        _conv_kernel,
        grid=(n_batch, B_WIDTH // tc),
        in_specs=[col(AB_BG), col(AB_CG), col(AB_BIN),
                  pl.BlockSpec((None, CONV_W - 1, tc), lambda bi, j: (bi, 0, j)),
                  pl.BlockSpec((CONV_W, tc), lambda bi, j: (0, j))],
        out_specs=[pl.BlockSpec((t, tc), lambda bi, j: (bi, j)),
                   pl.BlockSpec((None, CONV_W - 1, tc), lambda bi, j: (bi, 0, j))],
        out_shape=[jax.ShapeDtypeStruct((n_batch * t, B_WIDTH), BF16),
                   jax.ShapeDtypeStruct((n_batch, CONV_W - 1, B_WIDTH), F32)],
        compiler_params=_cparams(("parallel", "parallel")),
        name="short_conv",
    )(z, z, z, state, w)


def _cumsum_rows(x):
    n = x.shape[0]
    row = lax.broadcasted_iota(jnp.int32, x.shape, 0)
    s = 1
    while s < n:
        x = x + jnp.where(row >= s, pltpu.roll(x, s, 0), 0.0)
        s *= 2
    return x


def _bcast_rows(x, period, offset):
    n, w = x.shape
    return jnp.concatenate(
        [jnp.broadcast_to(x[b * period + offset:b * period + offset + 1, :], (period, w)) for b in range(n // period)],
        axis=0)


def _hgrn_kernel(q_ref, f_ref, v_ref, g_ref, lb_ref, gain_ref, s0_ref, o_ref, s_out_ref, st_scr, *, tc):
    step = pl.program_id(2)
    tb = q_ref.shape[0]
    nt = (((1,), (1,)), ((), ()))
    tn = (((0,), (0,)), ((), ()))

    @pl.when(step == 0)
    def _():
        st_scr[...] = s0_ref[...].T

    lbv = lb_ref[...]
    row = lax.broadcasted_iota(jnp.int32, (tc, C_DK), 0)
    trow = lax.broadcasted_iota(jnp.int32, (tc, tc), 0)
    tcol = lax.broadcasted_iota(jnp.int32, (tc, tc), 1)
    levels = [m for m in (32, 16, 8) if 2 * m <= tc]

    def chunk(c, carry):
        rows = pl.ds(pl.multiple_of(c * tc, tc), tc)
        f = lbv + (1.0 - lbv) * jax.nn.sigmoid(f_ref[rows, :])
        kk = 1.0 - f
        a = _cumsum_rows(jnp.log(f))
        qr = q_ref[rows, :]
        q = qr * jax.nn.sigmoid(qr)
        v = v_ref[rows, :].astype(BF16)
        st = st_scr[...]

        o = lax.dot_general((q * jnp.exp(a)).astype(BF16), st.astype(BF16), nt, preferred_element_type=F32)

        sc = jnp.zeros((tc, tc), F32)
        for m in levels:
            ref = _bcast_rows(a, 2 * m, m - 1)
            upper = jnp.bitwise_and(row, 2 * m - 1) >= m
            e = jnp.exp(jnp.where(upper, a - ref, ref - a))
            qm = jnp.where(upper, q * e, 0.0).astype(BF16)
            km = jnp.where(upper, 0.0, kk * e).astype(BF16)
            sm = lax.dot_general(qm, km, nt, preferred_element_type=F32)
            shift = (2 * m).bit_length() - 1
            same = jnp.right_shift(trow, shift) == jnp.right_shift(tcol, shift)
            sc = sc + jnp.where(same, sm, 0.0)
        for s in range(SUBLANES):
            ok = jnp.bitwise_and(row, SUBLANES - 1) >= s
            d = jnp.where(ok, a - _bcast_rows(a, SUBLANES, s), 0.0)
            w = q * _bcast_rows(kk, SUBLANES, s) * jnp.exp(d)
            col = jnp.sum(w, axis=-1, keepdims=True)
            here = jnp.logical_and(tcol == jnp.bitwise_and(trow, -SUBLANES) + s,
                                   jnp.bitwise_and(trow, SUBLANES - 1) >= s)
            sc = sc + jnp.where(here, col, 0.0)

        o = o + jnp.dot(sc.astype(BF16), v, preferred_element_type=F32)

        a_last = a[tc - 1:tc, :]
        kd = (kk * jnp.exp(a_last - a)).astype(BF16)
        st_scr[...] = st * jnp.exp(a_last) + lax.dot_general(v, kd, tn, preferred_element_type=F32)

        ms = jnp.mean(o * o, axis=-1, keepdims=True)
        gr = g_ref[rows, :]
        y = o * lax.rsqrt(ms + NORM_EPS) * gain_ref[...] * (gr * jax.nn.sigmoid(gr))
        o_ref[rows, :] = y.astype(o_ref.dtype)
        return carry

    lax.fori_loop(0, tb // tc, chunk, 0, unroll=min(HGRN_UNROLL, tb // tc))

    @pl.when(step == pl.num_programs(2) - 1)
    def _():
        s_out_ref[...] = st_scr[...].T


def _hgrn2(z, lb, gain, s0, *, n_batch, t, row0):
    tc = min(CHUNK, t)
    tb = min(512, t)
    nt = t // tb
    hb = CD_W // C_DK
    xspec = lambda seg: pl.BlockSpec((tb, C_DK), lambda bi, h, i: (row0 // tb + bi * nt + i, seg * hb + h))
    smap = lambda bi, h, i: (bi, h, 0, 0)
    return pl.pallas_call(
        functools.partial(_hgrn_kernel, tc=tc),
        grid=(n_batch, C_HEADS, nt),
        in_specs=[xspec(0), xspec(1), xspec(2), xspec(3),
                  pl.BlockSpec((1, C_DK), lambda bi, h, i: (0, h)),
                  pl.BlockSpec((1, C_DV), lambda bi, h, i: (0, 0)),
                  pl.BlockSpec((None, None, C_DK, C_DV), smap)],
        out_specs=[pl.BlockSpec((tb, C_DV), lambda bi, h, i: (bi * nt + i, h)),
                   pl.BlockSpec((None, None, C_DK, C_DV), smap)],
        out_shape=[jax.ShapeDtypeStruct((n_batch * t, CD_W), BF16), jax.ShapeDtypeStruct(s0.shape, F32)],
        scratch_shapes=[pltpu.VMEM((C_DV, C_DK), F32)],
        compiler_params=_cparams(("parallel", "parallel", "arbitrary")),
        name="hgrn2",
    )(z, z, z, z, lb, gain, s0)


def _prep_d_kernel(zq_ref, zk_ref, zf_ref, gq_ref, gk_ref, fb_ref, q_ref, k_ref, kb_ref, lf_ref):
    for h in range(D_HEADS):
        cols = slice(h * HEAD_DIM, (h + 1) * HEAD_DIM)
        q_ref[:, cols] = (_head_rms(zq_ref[:, cols], gq_ref[...]) * (HEAD_DIM ** -0.5)).astype(BF16)
        y = _head_rms(zk_ref[:, cols], gk_ref[...])
        k_ref[:, cols] = y
        kb_ref[:, cols] = y.astype(BF16)
    x = zf_ref[:, 0:LANES] + fb_ref[...]
    lf_ref[...] = jnp.minimum(x, 0.0) - jnp.log(1.0 + jnp.exp(-jnp.abs(x)))


def _prep_d(z, gq, gk, fb):
    m = z.shape[0]
    tm = _pick_tile(m, (256, 128, 96, 64, 32, 16))
    row = lambda w, c: pl.BlockSpec((tm, w), lambda i: (i, c))
    full = lambda w: pl.BlockSpec((tm, w), lambda i: (i, 0))
    par = pl.BlockSpec((1, LANES), lambda i: (0, 0))
    sds = lambda w, dt: jax.ShapeDtypeStruct((m, w), dt)
    return pl.pallas_call(
        _prep_d_kernel,
        grid=(m // tm,),
        in_specs=[row(CD_W, 4), row(CD_W, 5), row(N_ALIGN, CD_FD // N_ALIGN), par, par, par],
        out_specs=[full(CD_W), full(CD_W), full(CD_W), full(LANES)],
        out_shape=[sds(CD_W, BF16), sds(CD_W, F32), sds(CD_W, BF16), sds(LANES, F32)],
        compiler_params=_cparams(("parallel",)),
        name="prep_d",
    )(z, z, z, gq, gk, fb)


def _fox_kernel(q_ref, k_ref, v_ref, cq_ref, ck_ref, g_ref, o_ref, m_scr, l_scr, acc_scr):
    qi = pl.program_id(2)
    ki = pl.program_id(3)
    tq, tk = q_ref.shape[0], k_ref.shape[0]

    @pl.when(ki == 0)
    def _():
        m_scr[...] = jnp.full(m_scr.shape, -jnp.inf, F32)
        l_scr[...] = jnp.zeros(l_scr.shape, F32)
        acc_scr[...] = jnp.zeros(acc_scr.shape, F32)

    def step(masked):
        s = lax.dot_general(q_ref[...], k_ref[...], (((1,), (1,)), ((), ())), preferred_element_type=F32)
        s = s + cq_ref[...] - ck_ref[...]
        if masked:
            row = lax.broadcasted_iota(jnp.int32, (tq, tk), 0)
            col = lax.broadcasted_iota(jnp.int32, (tq, tk), 1)
            s = jnp.where(col <= row, s, -jnp.inf)
        m_prev = m_scr[...]
        m_new = jnp.maximum(m_prev, jnp.max(s, axis=-1, keepdims=True))
        alpha = jnp.exp(m_prev - m_new)
        p = jnp.exp(s - m_new)
        l_scr[...] = alpha * l_scr[...] + jnp.sum(p, axis=-1, keepdims=True)
        acc_scr[...] = alpha * acc_scr[...] + jnp.dot(p.astype(BF16), v_ref[...].astype(BF16),
                                                      preferred_element_type=F32)
        m_scr[...] = m_new

    @pl.when(ki < qi)
    def _():
        step(False)

    @pl.when(ki == qi)
    def _():
        step(True)
        o = acc_scr[...] / l_scr[...] * jax.nn.sigmoid(g_ref[...])
        o_ref[...] = o.astype(o_ref.dtype)


def _fox_prefill(q, k, z, cum_col, cum_row, *, n_batch, t):
    tq = min(512, t)
    nq = t // tq
    hb = CD_W // HEAD_DIM
    qmap = lambda bi, h, qi, ki: (bi * nq + qi, h)
    kmap = lambda bi, h, qi, ki: (bi * nq + jnp.minimum(ki, qi), h)
    return pl.pallas_call(
        _fox_kernel,
        grid=(n_batch, D_HEADS, nq, nq),
        in_specs=[pl.BlockSpec((tq, HEAD_DIM), qmap),
                  pl.BlockSpec((tq, HEAD_DIM), kmap),
                  pl.BlockSpec((tq, HEAD_DIM), lambda bi, h, qi, ki: (bi * nq + jnp.minimum(ki, qi), 6 * hb + h)),
                  pl.BlockSpec((None, None, tq, 1), lambda bi, h, qi, ki: (bi, h, qi, 0)),
                  pl.BlockSpec((None, None, 1, tq), lambda bi, h, qi, ki: (bi, h, 0, jnp.minimum(ki, qi))),
                  pl.BlockSpec((tq, HEAD_DIM), lambda bi, h, qi, ki: (bi * nq + qi, 7 * hb + h))],
        out_specs=pl.BlockSpec((tq, HEAD_DIM), qmap),
        out_shape=jax.ShapeDtypeStruct((n_batch * t, CD_W), BF16),
        scratch_shapes=[pltpu.VMEM((tq, 1), F32), pltpu.VMEM((tq, 1), F32), pltpu.VMEM((tq, HEAD_DIM), F32)],
        compiler_params=_cparams(("parallel", "parallel", "parallel", "arbitrary")),
        name="fox_prefill",
    )(q, k, z, cum_col, cum_row, z)


def _fox_dec_kernel(q_ref, kc_ref, vc_ref, kn_ref, vn_ref, cq_ref, ckc_ref, ckn_ref, g_ref, o_ref):
    t = q_ref.shape[0]
    q = q_ref[...]
    dn = (((1,), (1,)), ((), ()))
    s_c = lax.dot_general(q, kc_ref[...].astype(BF16), dn, preferred_element_type=F32)
    s_c = s_c + cq_ref[...] - ckc_ref[...]
    s_n = lax.dot_general(q, kn_ref[...], dn, preferred_element_type=F32)
    s_n = s_n + cq_ref[...] - ckn_ref[...]
    row = lax.broadcasted_iota(jnp.int32, (t, t), 0)
    col = lax.broadcasted_iota(jnp.int32, (t, t), 1)
    s_n = jnp.where(col <= row, s_n, -jnp.inf)
    m = jnp.maximum(jnp.max(s_c, axis=-1, keepdims=True), jnp.max(s_n, axis=-1, keepdims=True))
    p_c = jnp.exp(s_c - m)
    p_n = jnp.exp(s_n - m)
    l = jnp.sum(p_c, axis=-1, keepdims=True) + jnp.sum(p_n, axis=-1, keepdims=True)
    acc = jnp.dot(p_c.astype(BF16), vc_ref[...].astype(BF16), preferred_element_type=F32)
    acc = acc + jnp.dot(p_n.astype(BF16), vn_ref[...].astype(BF16), preferred_element_type=F32)
    o_ref[...] = (acc / l * jax.nn.sigmoid(g_ref[...])).astype(o_ref.dtype)


def _fox_decode(q, k, z, k_cache, v_cache, cq_col, ck_cache_row, ck_new_row, *, n_batch, t, row0):
    p = k_cache.shape[1]
    hb = CD_W // HEAD_DIM
    r0 = row0 // t
    new = lambda seg: pl.BlockSpec((t, HEAD_DIM), lambda bi, h: (r0 + bi, seg * hb + h))
    cache = pl.BlockSpec((None, p, HEAD_DIM), lambda bi, h: (bi, 0, h))
    return pl.pallas_call(
        _fox_dec_kernel,
        grid=(n_batch, D_HEADS),
        in_specs=[new(0), cache, cache, new(0), new(6),
                  pl.BlockSpec((None, None, t, 1), lambda bi, h: (bi, h, 0, 0)),
                  pl.BlockSpec((None, None, 1, p), lambda bi, h: (bi, h, 0, 0)),
                  pl.BlockSpec((None, None, 1, t), lambda bi, h: (bi, h, 0, 0)),
                  new(7)],
        out_specs=pl.BlockSpec((t, HEAD_DIM), lambda bi, h: (bi, h)),
        out_shape=jax.ShapeDtypeStruct((n_batch * t, CD_W), BF16),
        compiler_params=_cparams(("parallel", "parallel")),
        name="fox_decode",
    )(q, k_cache, v_cache, k, z, cq_col, ck_cache_row, ck_new_row, z)


def _pad_cols(w, n):
    return jnp.pad(w, ((0, 0), (0, n - w.shape[1])))


def _ab_in_weight(w):
    o = np.cumsum((0,) + AB_SIZES)
    seg = lambda i: w[:, o[i]:o[i + 1]]
    tail = _pad_cols(w[:, o[4]:o[6]], N_ALIGN)
    return jnp.concatenate([seg(0), seg(3), seg(1), seg(2), seg(6), seg(7), seg(8), tail], axis=1).astype(BF16)


def _cd_in_weight(w):
    o = np.cumsum((0,) + CD_SIZES)
    main = jnp.concatenate([w[:, :o[7]], w[:, o[8]:]], axis=1)
    tail = _pad_cols(w[:, o[7]:o[8]], N_ALIGN)
    return jnp.concatenate([main, tail], axis=1).astype(BF16)


def _expand_groups(mod_p, mod_s, seq_p):
    return jnp.concatenate([jnp.repeat(mod_p, seq_p // ROW_GROUP, axis=0), mod_s], axis=0)


def _lane_row(v):
    return jnp.pad(v.astype(F32), (0, LANES - v.shape[0]))[None, :]


def kernel(x_prompt, x_sample, c_prompt, c_sample, cache_a_k, cache_a_v, cache_a_idx_k, state_b_conv, state_c_s, cache_d_k, cache_d_v, cache_d_logf, ada_w, ada_b, norm_mix, norm_ffn, ab_w_in, ab_w_out, a_q_norm, a_k_norm, b_conv_w, cd_w_in, cd_w_out, c_lb, c_o_norm, d_q_norm, d_k_norm, d_f_bias, ffn_w_gate, ffn_w_up, ffn_w_down):
    bp, tp, d = x_prompt.shape
    bs, ts, _ = x_sample.shape
    past = cache_a_k.shape[2]
    depth = ada_w.shape[0]
    assert ts == ROW_GROUP and tp % ROW_GROUP == 0 and d == D_MODEL
    mp, ms = bp * tp, bs * ts

    x = jnp.concatenate([x_prompt.reshape(mp, d), x_sample.reshape(ms, d)], axis=0)
    c_all = jnp.concatenate([c_prompt, c_sample], axis=0)
    c_rows = -(-c_all.shape[0] // SUBLANES) * SUBLANES
    c_pad = jnp.pad(c_all, ((0, c_rows - c_all.shape[0]), (0, 0)))

    lb_all = jnp.cumsum(jax.nn.softmax(c_lb.astype(F32), axis=0), axis=0)
    lb_all = lb_all - lb_all[0]

    pos = jnp.concatenate([jnp.tile(jnp.arange(tp), bp), past + jnp.tile(jnp.arange(ts), bs)])
    cos_q, sin_q = _rope_tables(pos, HEAD_DIM)
    cos_i, sin_i = _rope_tables(pos, IDX_DIM)

    outs_p, outs_s = {}, {}
    for l in range(depth):
        mod = _ada_mod(c_pad, ada_w[l], ada_b[l][None, :])
        mods = [_expand_groups(m[:bp], m[bp:bp + bs], tp) for m in jnp.split(mod, 6, axis=-1)]
        sh1, sc1, g1, sh2, sc2, g2 = mods
        h = _norm_mod(x, norm_mix[l][None, :], sc1, sh1)
        if l % 2 == 0:
            e = l // 2
            z = _matmul(h, _ab_in_weight(ab_w_in[e]))
            q_b, qi_b, k_f, k_b, v_b, ki_f, ki_lo, ki_hi = _prep_a(
                z, cos_q, sin_q, cos_i, sin_i, a_q_norm[e][None, :], a_k_norm[e][None, :])
            v_f = z[:, AB_V:AB_V + A_KV_HEADS * HEAD_DIM]

            ya_p = _dsa_attention(qi_b, ki_lo, ki_hi, z, q_b, k_b, v_b, n_batch=bp, t=tp, tq=min(Q_BLOCK, tp),
                                  row0=0, lp=tp, q_pos0=0, l_valid=tp, n_sel=min(TOPK_MAX, tp // 4))
            ls = past + ts
            lp = -(-ls // LANES) * LANES

            def with_cache(cache, new_rows):
                cache = cache.reshape(bs, past, -1).astype(BF16)
                new_rows = new_rows[mp:].reshape(bs, ts, -1)
                full = jnp.concatenate([cache, new_rows], axis=1)
                return jnp.pad(full, ((0, 0), (0, lp - ls), (0, 0))).reshape(bs * lp, -1)

            zi = jnp.zeros_like(cache_a_idx_k[e])
            ya_s = _dsa_attention(
                qi_b, with_cache(jnp.concatenate([cache_a_idx_k[e], zi], -1), ki_lo),
                with_cache(jnp.concatenate([zi, cache_a_idx_k[e]], -1), ki_hi), z, q_b,
                with_cache(cache_a_k[e], k_b), with_cache(cache_a_v[e], v_b),
                n_batch=bs, t=ts, tq=ts, row0=mp, lp=lp, q_pos0=past, l_valid=ls, n_sel=min(TOPK_MAX, ls // 4))

            yb_p, conv_p = _short_conv(z, jnp.zeros((bp, CONV_W - 1, B_WIDTH), F32), b_conv_w[e],
                                       n_batch=bp, t=tp, row0=0)
            yb_s, conv_s = _short_conv(z, state_b_conv[e], b_conv_w[e], n_batch=bs, t=ts, row0=mp)
            x = _mix_out(jnp.concatenate([ya_p, ya_s], axis=0), jnp.concatenate([yb_p, yb_s], axis=0),
                         ab_w_out[e], x, g1)

            for od, sl, bn, t, conv_new in ((outs_p, slice(0, mp), bp, tp, conv_p),
                                            (outs_s, slice(mp, mp + ms), bs, ts, conv_s)):
                od.setdefault("a_k", []).append(k_f[sl].reshape(bn, t, A_KV_HEADS, HEAD_DIM))
                od.setdefault("a_v", []).append(v_f[sl].reshape(bn, t, A_KV_HEADS, HEAD_DIM))
                od.setdefault("a_ik", []).append(ki_f[sl].reshape(bn, t, IDX_DIM))
                od.setdefault("b_conv", []).append(conv_new)
        else:
            od_ = l // 2
            z = _matmul(h, _cd_in_weight(cd_w_in[od_]))
            q_b, k_f, k_b, lf_slab = _prep_d(z, d_q_norm[od_][None, :], d_k_norm[od_][None, :],
                                             _lane_row(d_f_bias[od_]))
            logf = lf_slab[:, :D_HEADS]
            v_f = z[:, 6 * CD_W:7 * CD_W]
            lb = lb_all[l][None, :]
            gain = c_o_norm[od_][None, :]

            yc_p, s_p = _hgrn2(z, lb, gain, jnp.zeros((bp, C_HEADS, C_DK, C_DV), F32), n_batch=bp, t=tp, row0=0)
            yc_s, s_s = _hgrn2(z, lb, gain, state_c_s[od_], n_batch=bs, t=ts, row0=mp)

            cum_p = jnp.cumsum(logf[:mp].reshape(bp, tp, D_HEADS), axis=1).transpose(0, 2, 1)
            yd_p = _fox_prefill(q_b, k_b, z, cum_p[..., None], cum_p[:, :, None, :], n_batch=bp, t=tp)
            cum_s = jnp.cumsum(jnp.concatenate([cache_d_logf[od_], logf[mp:].reshape(bs, ts, D_HEADS)], axis=1),
                               axis=1).transpose(0, 2, 1)
            yd_s = _fox_decode(q_b, k_b, z, cache_d_k[od_].reshape(bs, past, CD_W),
                               cache_d_v[od_].reshape(bs, past, CD_W), cum_s[:, :, past:, None],
                               cum_s[:, :, None, :past], cum_s[:, :, None, past:], n_batch=bs, t=ts, row0=mp)
            x = _mix_out(jnp.concatenate([yc_p, yc_s], axis=0), jnp.concatenate([yd_p, yd_s], axis=0),
                         cd_w_out[od_], x, g1)

            for od, sl, bn, t, s_new in ((outs_p, slice(0, mp), bp, tp, s_p),
                                         (outs_s, slice(mp, mp + ms), bs, ts, s_s)):
                od.setdefault("c_s", []).append(s_new)
                od.setdefault("d_k", []).append(k_f[sl].reshape(bn, t, D_HEADS, HEAD_DIM))
                od.setdefault("d_v", []).append(v_f[sl].reshape(bn, t, D_HEADS, HEAD_DIM))
                od.setdefault("d_lf", []).append(logf[sl].reshape(bn, t, D_HEADS))
        h2 = _norm_mod(x, norm_ffn[l][None, :], sc2, sh2)
        hid = _ffn_gate_up(h2, ffn_w_gate[l], ffn_w_up[l])
        x = _ffn_down(hid, ffn_w_down[l], x, g2)

    y_prompt = x[:mp].reshape(bp, tp, d)
    y_sample = x[mp:].reshape(bs, ts, d)
    names = ("a_k", "a_v", "a_ik", "b_conv", "c_s", "d_k", "d_v", "d_lf")
    return ((y_prompt, y_sample) + tuple(jnp.stack(outs_p[n]) for n in names)
            + tuple(jnp.stack(outs_s[n]) for n in names))
```

```python
import functools

import numpy as np
import jax
import jax.numpy as jnp
from jax import lax
from jax.experimental import pallas as pl
from jax.experimental.pallas import tpu as pltpu

F32 = jnp.float32
BF16 = jnp.bfloat16

D_MODEL = 4096
CHUNK = 64
Q_BLOCK = 128
HEAD_DIM = 128
ROPE_THETA = 500000.0
ROT_FRACTION = 4
NORM_EPS = 1e-6
A_HEADS = D_MODEL // 256
A_KV_HEADS = A_HEADS // 4
IDX_HEADS = D_MODEL // 128
IDX_DIM = 64
TOPK_MAX = 256
B_WIDTH = D_MODEL // 2
CONV_W = 3
C_HEADS = D_MODEL // 256
C_DK = 128
C_DV = 128
D_HEADS = D_MODEL // 256
AB_SIZES = (A_HEADS * HEAD_DIM, A_KV_HEADS * HEAD_DIM, A_KV_HEADS * HEAD_DIM,
            IDX_HEADS * IDX_DIM, IDX_DIM, IDX_HEADS, B_WIDTH, B_WIDTH, B_WIDTH)
CD_SIZES = (C_HEADS * C_DK, C_HEADS * C_DK, C_HEADS * C_DV, C_HEADS * C_DV,
            D_HEADS * HEAD_DIM, D_HEADS * HEAD_DIM, D_HEADS * HEAD_DIM, D_HEADS, D_HEADS * HEAD_DIM)

LANES = 128
SUBLANES = 8
VMEM_LIMIT_BYTES = 56 * 1024 * 1024

ROW_GROUP = 16
N_ALIGN = 512
INT_MIN = -2 ** 31
HGRN_UNROLL = 4
FOX_TQ = 256
DSA_GROUP = 512

AB_Q, AB_K, AB_V, AB_QI, AB_HEAD = 0, 2048, 2560, 3072, 5120
AB_BG, AB_CG, AB_BIN, AB_KW = 0, 2048, 4096, 6144
CD_W = 2048
CD_HEAD = 7 * CD_W
CD_FD = CD_W


def _cparams(sem):
    return pltpu.CompilerParams(dimension_semantics=sem, vmem_limit_bytes=VMEM_LIMIT_BYTES)


def _pick_tile(n, candidates):
    for c in candidates:
        if n % c == 0:
            return c
    raise ValueError(f"no tile for {n}")


M_TILES = (768, 512, 256, 128, 96, 64, 32, 16)


def _ada_kernel(c_ref, w_ref, b_ref, o_ref):
    c = c_ref[...]
    a = (c * jax.nn.sigmoid(c)).astype(BF16)
    o_ref[...] = jnp.dot(a, w_ref[...].astype(BF16), preferred_element_type=F32) + b_ref[...]


def _ada_mod(c, w, b):
    r, d = c.shape
    depth, _, n = w.shape
    tn = _pick_tile(n, (512, 256, 128))
    return pl.pallas_call(
        _ada_kernel,
        grid=(depth, n // tn),
        in_specs=[pl.BlockSpec((r, d), lambda l, j: (0, 0)),
                  pl.BlockSpec((None, d, tn), lambda l, j: (l, 0, j)),
                  pl.BlockSpec((None, 1, tn), lambda l, j: (l, 0, j))],
        out_specs=pl.BlockSpec((None, r, tn), lambda l, j: (l, 0, j)),
        out_shape=jax.ShapeDtypeStruct((depth, r, n), F32),
        compiler_params=_cparams(("parallel", "parallel")),
        name="ada_mod",
    )(c, w, b)


def _norm_mod_kernel(x_ref, gain_ref, sc_ref, sh_ref, o_ref, *, groups):
    gain = gain_ref[...]
    for g in range(groups):
        rows = slice(g * ROW_GROUP, (g + 1) * ROW_GROUP)
        x = x_ref[rows, :]
        ms = jnp.mean(x * x, axis=-1, keepdims=True)
        y = x * lax.rsqrt(ms + NORM_EPS) * gain
        o_ref[rows, :] = (y * (1.0 + sc_ref[g:g + 1, :]) + sh_ref[g:g + 1, :]).astype(o_ref.dtype)


def _norm_mod(x, gain, sc, sh):
    m, d = x.shape
    tm = _pick_tile(m, (384,) + M_TILES)
    groups = tm // ROW_GROUP
    return pl.pallas_call(
        functools.partial(_norm_mod_kernel, groups=groups),
        grid=(m // tm,),
        in_specs=[pl.BlockSpec((tm, d), lambda i: (i, 0)),
                  pl.BlockSpec((1, d), lambda i: (0, 0)),
                  pl.BlockSpec((groups, d), lambda i: (i, 0)),
                  pl.BlockSpec((groups, d), lambda i: (i, 0))],
        out_specs=pl.BlockSpec((tm, d), lambda i: (i, 0)),
        out_shape=jax.ShapeDtypeStruct((m, d), BF16),
        compiler_params=_cparams(("parallel",)),
        name="norm_mod",
    )(x, gain, sc, sh)


def _mm_kernel(a_ref, w_ref, o_ref):
    o_ref[...] = jnp.dot(a_ref[...], w_ref[...], preferred_element_type=F32).astype(o_ref.dtype)


def _matmul(a, w):
    m, k = a.shape
    n = w.shape[1]
    tm = _pick_tile(m, M_TILES)
    tn = _pick_tile(n, (512, 256, 128))
    return pl.pallas_call(
        _mm_kernel,
        grid=(m // tm, n // tn),
        in_specs=[pl.BlockSpec((tm, k), lambda i, j: (i, 0)),
                  pl.BlockSpec((k, tn), lambda i, j: (0, j))],
        out_specs=pl.BlockSpec((tm, tn), lambda i, j: (i, j)),
        out_shape=jax.ShapeDtypeStruct((m, n), F32),
        compiler_params=_cparams(("parallel", "parallel")),
        name="proj_in",
    )(a, w)


def _mm_w32_kernel(a_ref, w_ref, o_ref):
    o_ref[...] = jnp.dot(a_ref[...], w_ref[...].astype(BF16), preferred_element_type=F32)


def _matmul_head(a, w, layer, n_cols):
    m, k = a.shape
    tm = _pick_tile(m, M_TILES)
    tn = _pick_tile(n_cols, (512, 256, 128))
    return pl.pallas_call(
        _mm_w32_kernel,
        grid=(m // tm, n_cols // tn),
        in_specs=[pl.BlockSpec((tm, k), lambda i, j: (i, 0)),
                  pl.BlockSpec((None, k, tn), lambda i, j: (layer, 0, j))],
        out_specs=pl.BlockSpec((tm, tn), lambda i, j: (i, j)),
        out_shape=jax.ShapeDtypeStruct((m, n_cols), F32),
        compiler_params=_cparams(("parallel", "parallel")),
        name="proj_in_head",
    )(a, w)


def _gated_residual(o_ref, x_ref, g_ref, acc, groups):
    for g in range(groups):
        rows = slice(g * ROW_GROUP, (g + 1) * ROW_GROUP)
        o_ref[rows, :] = x_ref[rows, :] + g_ref[g:g + 1, :] * acc[rows, :]


def _mix_out_kernel(a1_ref, a2_ref, w1_ref, w2_ref, x_ref, g_ref, o_ref, *, groups):
    acc = jnp.dot(a1_ref[...], w1_ref[...].astype(BF16), preferred_element_type=F32)
    acc = acc + jnp.dot(a2_ref[...], w2_ref[...].astype(BF16), preferred_element_type=F32)
    _gated_residual(o_ref, x_ref, g_ref, acc, groups)


def _mix_out(a1, a2, w, x, gate):
    m, kh = a1.shape
    n = w.shape[1]
    assert w.shape[0] == 2 * kh
    tm = _pick_tile(m, M_TILES)
    tn = _pick_tile(n, (512, 256, 128))
    groups = tm // ROW_GROUP
    return pl.pallas_call(
        functools.partial(_mix_out_kernel, groups=groups),
        grid=(m // tm, n // tn),
        in_specs=[pl.BlockSpec((tm, kh), lambda i, j: (i, 0)),
                  pl.BlockSpec((tm, kh), lambda i, j: (i, 0)),
                  pl.BlockSpec((kh, tn), lambda i, j: (0, j)),
                  pl.BlockSpec((kh, tn), lambda i, j: (1, j)),
                  pl.BlockSpec((tm, tn), lambda i, j: (i, j)),
                  pl.BlockSpec((groups, tn), lambda i, j: (i, j))],
        out_specs=pl.BlockSpec((tm, tn), lambda i, j: (i, j)),
        out_shape=jax.ShapeDtypeStruct((m, n), F32),
        compiler_params=_cparams(("parallel", "parallel")),
        name="mix_out_resid",
    )(a1, a2, w, w, x, gate)


def _ffn_down_kernel(a_ref, w_ref, x_ref, g_ref, o_ref, acc_ref, *, nk, groups):
    k = pl.program_id(2)
    part = jnp.dot(a_ref[...], w_ref[...], preferred_element_type=F32)
    if nk == 1:
        _gated_residual(o_ref, x_ref, g_ref, part, groups)
    else:
        @pl.when(k == 0)
        def _():
            acc_ref[...] = part

        @pl.when(jnp.logical_and(k > 0, k < nk - 1))
        def _():
            acc_ref[...] += part

        @pl.when(k == nk - 1)
        def _():
            _gated_residual(o_ref, x_ref, g_ref, acc_ref[...] + part, groups)


def _ffn_down(a, w, x, gate):
    m, k = a.shape
    n = w.shape[1]
    tm = _pick_tile(m, M_TILES)
    tn = _pick_tile(n, (512, 256, 128))
    tk = k if k <= 4096 else _pick_tile(k, (5504, 4096, 2048, 1024))
    nk = k // tk
    groups = tm // ROW_GROUP
    return pl.pallas_call(
        functools.partial(_ffn_down_kernel, nk=nk, groups=groups),
        grid=(m // tm, n // tn, nk),
        in_specs=[pl.BlockSpec((tm, tk), lambda i, j, kk: (i, kk)),
                  pl.BlockSpec((tk, tn), lambda i, j, kk: (kk, j)),
                  pl.BlockSpec((tm, tn), lambda i, j, kk: (i, j)),
                  pl.BlockSpec((groups, tn), lambda i, j, kk: (i, j))],
        out_specs=pl.BlockSpec((tm, tn), lambda i, j, kk: (i, j)),
        out_shape=jax.ShapeDtypeStruct((m, n), F32),
        scratch_shapes=[pltpu.VMEM((tm, tn), F32)],
        compiler_params=_cparams(("parallel", "parallel", "arbitrary")),
        name="ffn_down_resid",
    )(a, w, x, gate)


def _ffn_gate_up_kernel(a_ref, wg_ref, wu_ref, o_ref):
    a = a_ref[...]
    g = jnp.dot(a, wg_ref[...].astype(BF16), preferred_element_type=F32)
    u = jnp.dot(a, wu_ref[...].astype(BF16), preferred_element_type=F32)
    o_ref[...] = (g * jax.nn.sigmoid(g) * u).astype(o_ref.dtype)


def _ffn_gate_up(a, wg, wu, layer):
    m, k = a.shape
    n = wg.shape[2]
    tm = _pick_tile(m, (1056,) + M_TILES)
    tn = _pick_tile(n, (256, 128))
    return pl.pallas_call(
        _ffn_gate_up_kernel,
        grid=(m // tm, n // tn),
        in_specs=[pl.BlockSpec((tm, k), lambda i, j: (i, 0)),
                  pl.BlockSpec((None, k, tn), lambda i, j: (layer, 0, j)),
                  pl.BlockSpec((None, k, tn), lambda i, j: (layer, 0, j))],
        out_specs=pl.BlockSpec((tm, tn), lambda i, j: (i, j)),
        out_shape=jax.ShapeDtypeStruct((m, n), BF16),
        compiler_params=_cparams(("parallel", "parallel")),
        name="ffn_gate_up",
    )(a, wg, wu)


def _rope_lanes(y, cos, sin_signed, half, period):
    lane = lax.broadcasted_iota(jnp.int32, y.shape, 1)
    first = jnp.bitwise_and(lane, period - 1) < half
    swap = jnp.where(first, pltpu.roll(y, LANES - half, 1), pltpu.roll(y, half, 1))
    return y * cos + swap * sin_signed


def _head_rms(x, gain):
    ms = jnp.mean(x * x, axis=-1, keepdims=True)
    return x * lax.rsqrt(ms + NORM_EPS) * gain


def _prep_a_kernel(zq_ref, zqi0_ref, zqi1_ref, zkv_ref, zkw_ref, cq_ref, sq_ref, ci_ref, si_ref, gq_ref, gk_ref,
                   q_ref, qi_ref, k_ref, kb_ref, vb_ref, ki_ref, kilo_ref, kihi_ref):
    cq, sq, ci, si = cq_ref[...], sq_ref[...], ci_ref[...], si_ref[...]
    half_q = HEAD_DIM // ROT_FRACTION // 2
    half_i = IDX_DIM // ROT_FRACTION // 2
    for h in range(A_HEADS):
        cols = slice(h * HEAD_DIM, (h + 1) * HEAD_DIM)
        y = _rope_lanes(_head_rms(zq_ref[:, cols], gq_ref[...]), cq, sq, half_q, HEAD_DIM)
        q_ref[:, cols] = (y * (HEAD_DIM ** -0.5)).astype(BF16)
    for h in range(A_KV_HEADS):
        cols = slice(h * HEAD_DIM, (h + 1) * HEAD_DIM)
        y = _rope_lanes(_head_rms(zkv_ref[:, cols], gk_ref[...]), cq, sq, half_q, HEAD_DIM)
        k_ref[:, cols] = y
        kb_ref[:, cols] = y.astype(BF16)
    nkv = A_KV_HEADS * HEAD_DIM
    vb_ref[...] = zkv_ref[:, nkv:2 * nkv].astype(BF16)
    slabs = zqi0_ref.shape[1] // LANES
    for j in range(IDX_HEADS * IDX_DIM // LANES):
        src = zqi0_ref if j < slabs else zqi1_ref
        x = src[:, (j % slabs) * LANES:(j % slabs + 1) * LANES]
        qi_ref[:, j * LANES:(j + 1) * LANES] = _rope_lanes(x, ci, si, half_i, IDX_DIM).astype(BF16)
    slab = _rope_lanes(zkw_ref[:, 0:LANES], ci, si, half_i, IDX_DIM)
    ki_ref[...] = slab[:, 0:IDX_DIM]
    lane = lax.broadcasted_iota(jnp.int32, slab.shape, 1)
    lo = jnp.where(lane < IDX_DIM, slab, 0.0)
    kilo_ref[...] = lo.astype(BF16)
    kihi_ref[...] = pltpu.roll(lo, IDX_DIM, 1).astype(BF16)


def _prep_a(za, zb, cq, sq, ci, si, gq, gk):
    m = za.shape[0]
    tm = _pick_tile(m, (256, 128, 96, 64, 32, 16))
    row = lambda w, c0: pl.BlockSpec((tm, w), lambda i: (i, c0 // w))
    full = lambda w: pl.BlockSpec((tm, w), lambda i: (i, 0))
    par = pl.BlockSpec((1, LANES), lambda i: (0, 0))
    nq, nkv = A_HEADS * HEAD_DIM, A_KV_HEADS * HEAD_DIM
    nqi = IDX_HEADS * IDX_DIM // 2
    sds = lambda w, dt: jax.ShapeDtypeStruct((m, w), dt)
    return pl.pallas_call(
        _prep_a_kernel,
        grid=(m // tm,),
        in_specs=[row(nq, AB_Q), row(nqi, AB_QI), row(nqi, AB_QI + nqi), row(2 * nkv, AB_K),
                  row(N_ALIGN, AB_KW), full(LANES), full(LANES), full(LANES), full(LANES), par, par],
        out_specs=[full(nq), full(nq), full(nkv), full(nkv), full(nkv), full(IDX_DIM), full(LANES), full(LANES)],
        out_shape=[sds(nq, BF16), sds(nq, BF16), sds(nkv, F32), sds(nkv, BF16), sds(nkv, BF16),
                   sds(IDX_DIM, F32), sds(LANES, BF16), sds(LANES, BF16)],
        compiler_params=_cparams(("parallel",)),
        name="prep_a",
    )(za, za, za, za, zb, cq, sq, ci, si, gq, gk)


def _rope_tables(pos, head_dim):
    r = head_dim // ROT_FRACTION
    half = r // 2
    inv = 1.0 / (ROPE_THETA ** (jnp.arange(half, dtype=F32) / half))
    ang = pos.astype(F32)[:, None] * inv[None, :]
    cos, sin = jnp.cos(ang), jnp.sin(ang)
    lane = np.arange(LANES) % head_dim
    idx = lane % half
    in_rot = jnp.asarray(lane < r)[None, :]
    first = jnp.asarray(lane < half)[None, :]
    c = jnp.where(in_rot, cos[:, idx], 1.0)
    s = jnp.where(first, -sin[:, idx], jnp.where(in_rot, sin[:, idx], 0.0))
    return c, s


def _dsa_kernel(qi_ref, kilo_ref, kihi_ref, wi_ref, q_ref, k_ref, v_ref, o_ref, key_scr, bias_scr, *,
                q_pos0, l_valid, n_sel):
    tq = q_ref.shape[0]
    lp = k_ref.shape[0]
    dn = (((1,), (1,)), ((), ()))

    kilo = kilo_ref[...]
    kihi = kihi_ref[...]
    wi = wi_ref[...] * ((IDX_HEADS * IDX_DIM) ** -0.5)
    score = jnp.zeros((tq, lp), F32)
    for j in range(IDX_HEADS // 2):
        qi2 = qi_ref[:, j * LANES:(j + 1) * LANES]
        s_lo = lax.dot_general(qi2, kilo, dn, preferred_element_type=F32)
        s_hi = lax.dot_general(qi2, kihi, dn, preferred_element_type=F32)
        c = IDX_DIM + 2 * j
        score = score + jnp.maximum(s_lo, 0.0) * wi[:, c:c + 1]
        score = score + jnp.maximum(s_hi, 0.0) * wi[:, c + 1:c + 2]

    q_pos = q_pos0 + pl.program_id(1) * tq + lax.broadcasted_iota(jnp.int32, (tq, lp), 0)
    k_pos = lax.broadcasted_iota(jnp.int32, (tq, lp), 1)
    shift = CHUNK.bit_length() - 1
    adm = jnp.logical_and(jnp.right_shift(k_pos, shift) <= jnp.right_shift(q_pos, shift), k_pos < l_valid)
    bits = pltpu.bitcast(score, jnp.int32)
    key = jnp.where(bits < 0, jnp.bitwise_xor(bits, jnp.int32(0x7FFFFFFF)), bits)
    key_scr[...] = jnp.where(adm, key, jnp.int32(INT_MIN))

    def count_ge(t):
        return jnp.sum(jnp.where(key_scr[...] >= t, 1.0, 0.0), axis=-1, keepdims=True)

    zero = jnp.zeros((tq, 1), jnp.int32)
    t0 = jnp.where(count_ge(zero) >= n_sel, zero, jnp.int32(INT_MIN))

    def body(i, t):
        cand = jnp.bitwise_or(t, jnp.left_shift(jnp.int32(1), 30 - i))
        return jnp.where(count_ge(cand) >= n_sel, cand, t)

    thr = lax.fori_loop(0, 31, body, t0)
    sel = jnp.logical_and(key_scr[...] >= thr, adm)
    bias_scr[...] = jnp.where(sel, 0.0, -jnp.inf)

    group = A_HEADS // A_KV_HEADS
    for kvh in range(A_KV_HEADS):
        kk = k_ref[:, kvh * HEAD_DIM:(kvh + 1) * HEAD_DIM]
        vv = v_ref[:, kvh * HEAD_DIM:(kvh + 1) * HEAD_DIM]
        for g in range(group):
            cols = slice((kvh * group + g) * HEAD_DIM, (kvh * group + g + 1) * HEAD_DIM)
            s = lax.dot_general(q_ref[:, cols], kk, dn, preferred_element_type=F32) + bias_scr[...]
            m = jnp.max(s, axis=-1, keepdims=True)
            p = jnp.exp(s - m)
            l = jnp.sum(p, axis=-1, keepdims=True)
            o = jnp.dot(p.astype(BF16), vv, preferred_element_type=F32) / l
            o_ref[:, cols] = o.astype(o_ref.dtype)


def _dsa_attention(qi, ki_lo, ki_hi, z, q, k, v, *, n_batch, t, tq, row0, q_stride, lp, q_pos0, l_valid, n_sel):
    nq = t // tq
    qrow = lambda bi, i: (row0 + bi * q_stride) // tq + i
    qmap = lambda bi, i: (qrow(bi, i), 0)
    kmap = lambda bi, i: (bi, 0, 0)
    return pl.pallas_call(
        functools.partial(_dsa_kernel, q_pos0=q_pos0, l_valid=l_valid, n_sel=n_sel),
        grid=(n_batch, nq),
        in_specs=[pl.BlockSpec((tq, qi.shape[1]), qmap),
                  pl.BlockSpec((None, lp, LANES), kmap),
                  pl.BlockSpec((None, lp, LANES), kmap),
                  pl.BlockSpec((tq, LANES), lambda bi, i: (qrow(bi, i), AB_KW // LANES)),
                  pl.BlockSpec((tq, q.shape[1]), qmap),
                  pl.BlockSpec((None, lp, k.shape[2]), kmap),
                  pl.BlockSpec((None, lp, v.shape[2]), kmap)],
        out_specs=pl.BlockSpec((tq, q.shape[1]), lambda bi, i: (bi * nq + i, 0)),
        out_shape=jax.ShapeDtypeStruct((n_batch * t, q.shape[1]), BF16),
        scratch_shapes=[pltpu.VMEM((tq, lp), jnp.int32), pltpu.VMEM((tq, lp), F32)],
        compiler_params=_cparams(("parallel", "arbitrary")),
        name="dsa_attention",
    )(qi, ki_lo, ki_hi, z, q, k, v)


def _conv_kernel(bg_ref, cg_ref, u_ref, st_ref, w_ref, y_ref, ns_ref):
    t = u_ref.shape[0]
    u = cg_ref[...] * u_ref[...]
    st = st_ref[...]
    row = lax.broadcasted_iota(jnp.int32, u.shape, 0)
    u1 = jnp.where(row == 0, st[1:2, :], pltpu.roll(u, 1, 0))
    u2 = jnp.where(row == 0, st[0:1, :], jnp.where(row == 1, st[1:2, :], pltpu.roll(u, 2, 0)))
    w = w_ref[...]
    y = u2 * w[0:1, :] + u1 * w[1:2, :] + u * w[2:3, :]
    y_ref[...] = (bg_ref[...] * y).astype(y_ref.dtype)
    ns_ref[...] = u[t - (CONV_W - 1):t, :]


def _short_conv(z, state, w, *, n_batch, t, row0):
    tc = 256
    nb = row0 // t
    col = lambda c0: pl.BlockSpec((t, tc), lambda bi, j: (nb + bi, c0 // tc + j))
    return pl.pallas_call(
        _conv_kernel,
        grid=(n_batch, B_WIDTH // tc),
        in_specs=[col(AB_BG), col(AB_CG), col(AB_BIN),
                  pl.BlockSpec((None, CONV_W - 1, tc), lambda bi, j: (bi, 0, j)),
                  pl.BlockSpec((CONV_W, tc), lambda bi, j: (0, j))],
        out_specs=[pl.BlockSpec((t, tc), lambda bi, j: (bi, j)),
                   pl.BlockSpec((None, CONV_W - 1, tc), lambda bi, j: (bi, 0, j))],
        out_shape=[jax.ShapeDtypeStruct((n_batch * t, B_WIDTH), BF16),
                   jax.ShapeDtypeStruct((n_batch, CONV_W - 1, B_WIDTH), F32)],
        compiler_params=_cparams(("parallel", "parallel")),
        name="short_conv",
    )(z, z, z, state, w)


def _cumsum_rows(x):
    n = x.shape[0]
    row = lax.broadcasted_iota(jnp.int32, x.shape, 0)
    s = 1
    while s < n:
        x = x + jnp.where(row >= s, pltpu.roll(x, s, 0), 0.0)
        s *= 2
    return x


def _bcast_rows(x, period, offset):
    n, w = x.shape
    return jnp.concatenate(
        [jnp.broadcast_to(x[b * period + offset:b * period + offset + 1, :], (period, w)) for b in range(n // period)],
        axis=0)


def _hgrn_kernel(q_ref, f_ref, v_ref, g_ref, lb_ref, gain_ref, s0_ref, o_ref, s_out_ref, st_scr, *, tc):
    step = pl.program_id(2)
    tb = q_ref.shape[0]
    nt = (((1,), (1,)), ((), ()))
    tn = (((0,), (0,)), ((), ()))

    @pl.when(step == 0)
    def _():
        st_scr[...] = s0_ref[...].T

    lbv = lb_ref[...]
    row = lax.broadcasted_iota(jnp.int32, (tc, C_DK), 0)
    trow = lax.broadcasted_iota(jnp.int32, (tc, tc), 0)
    tcol = lax.broadcasted_iota(jnp.int32, (tc, tc), 1)
    levels = [m for m in (32, 16, 8) if 2 * m <= tc]

    def chunk(c, carry):
        rows = pl.ds(pl.multiple_of(c * tc, tc), tc)
        f = lbv + (1.0 - lbv) * jax.nn.sigmoid(f_ref[rows, :])
        kk = 1.0 - f
        a = _cumsum_rows(jnp.log(f))
        qr = q_ref[rows, :]
        q = qr * jax.nn.sigmoid(qr)
        v = v_ref[rows, :].astype(BF16)
        st = st_scr[...]

        o = lax.dot_general((q * jnp.exp(a)).astype(BF16), st.astype(BF16), nt, preferred_element_type=F32)

        sc = jnp.zeros((tc, tc), F32)
        for m in levels:
            ref = _bcast_rows(a, 2 * m, m - 1)
            upper = jnp.bitwise_and(row, 2 * m - 1) >= m
            e = jnp.exp(jnp.where(upper, a - ref, ref - a))
            qm = jnp.where(upper, q * e, 0.0).astype(BF16)
            km = jnp.where(upper, 0.0, kk * e).astype(BF16)
            sm = lax.dot_general(qm, km, nt, preferred_element_type=F32)
            shift = (2 * m).bit_length() - 1
            same = jnp.right_shift(trow, shift) == jnp.right_shift(tcol, shift)
            sc = sc + jnp.where(same, sm, 0.0)
        for s in range(SUBLANES):
            ok = jnp.bitwise_and(row, SUBLANES - 1) >= s
            d = jnp.where(ok, a - _bcast_rows(a, SUBLANES, s), 0.0)
            w = q * _bcast_rows(kk, SUBLANES, s) * jnp.exp(d)
            col = jnp.sum(w, axis=-1, keepdims=True)
            here = jnp.logical_and(tcol == jnp.bitwise_and(trow, -SUBLANES) + s,
                                   jnp.bitwise_and(trow, SUBLANES - 1) >= s)
            sc = sc + jnp.where(here, col, 0.0)

        o = o + jnp.dot(sc.astype(BF16), v, preferred_element_type=F32)

        a_last = a[tc - 1:tc, :]
        kd = (kk * jnp.exp(a_last - a)).astype(BF16)
        st_scr[...] = st * jnp.exp(a_last) + lax.dot_general(v, kd, tn, preferred_element_type=F32)

        ms = jnp.mean(o * o, axis=-1, keepdims=True)
        gr = g_ref[rows, :]
        y = o * lax.rsqrt(ms + NORM_EPS) * gain_ref[...] * (gr * jax.nn.sigmoid(gr))
        o_ref[rows, :] = y.astype(o_ref.dtype)
        return carry

    lax.fori_loop(0, tb // tc, chunk, 0, unroll=min(HGRN_UNROLL, tb // tc))

    @pl.when(step == pl.num_programs(2) - 1)
    def _():
        s_out_ref[...] = st_scr[...].T


def _hgrn2(z, lb, gain, s0, *, n_batch, t, row0):
    tc = min(CHUNK, t)
    tb = min(512, t)
    nt = t // tb
    hb = CD_W // C_DK
    xspec = lambda seg: pl.BlockSpec((tb, C_DK), lambda bi, h, i: (row0 // tb + bi * nt + i, seg * hb + h))
    smap = lambda bi, h, i: (bi, h, 0, 0)
    return pl.pallas_call(
        functools.partial(_hgrn_kernel, tc=tc),
        grid=(n_batch, C_HEADS, nt),
        in_specs=[xspec(0), xspec(1), xspec(2), xspec(3),
                  pl.BlockSpec((1, C_DK), lambda bi, h, i: (0, h)),
                  pl.BlockSpec((1, C_DV), lambda bi, h, i: (0, 0)),
                  pl.BlockSpec((None, None, C_DK, C_DV), smap)],
        out_specs=[pl.BlockSpec((tb, C_DV), lambda bi, h, i: (bi * nt + i, h)),
                   pl.BlockSpec((None, None, C_DK, C_DV), smap)],
        out_shape=[jax.ShapeDtypeStruct((n_batch * t, CD_W), BF16), jax.ShapeDtypeStruct(s0.shape, F32)],
        scratch_shapes=[pltpu.VMEM((C_DV, C_DK), F32)],
        compiler_params=_cparams(("parallel", "parallel", "arbitrary")),
        name="hgrn2",
    )(z, z, z, z, lb, gain, s0)


def _prep_d_kernel(zq_ref, zk_ref, zf_ref, gq_ref, gk_ref, fb_ref, q_ref, k_ref, kb_ref, lf_ref):
    for h in range(D_HEADS):
        cols = slice(h * HEAD_DIM, (h + 1) * HEAD_DIM)
        q_ref[:, cols] = (_head_rms(zq_ref[:, cols], gq_ref[...]) * (HEAD_DIM ** -0.5)).astype(BF16)
        y = _head_rms(zk_ref[:, cols], gk_ref[...])
        k_ref[:, cols] = y
        kb_ref[:, cols] = y.astype(BF16)
    x = zf_ref[:, 0:LANES] + fb_ref[...]
    lf_ref[...] = jnp.minimum(x, 0.0) - jnp.log(1.0 + jnp.exp(-jnp.abs(x)))


def _prep_d(za, zb, gq, gk, fb):
    m = za.shape[0]
    tm = _pick_tile(m, (256, 128, 96, 64, 32, 16))
    row = lambda w, c: pl.BlockSpec((tm, w), lambda i: (i, c))
    full = lambda w: pl.BlockSpec((tm, w), lambda i: (i, 0))
    par = pl.BlockSpec((1, LANES), lambda i: (0, 0))
    sds = lambda w, dt: jax.ShapeDtypeStruct((m, w), dt)
    return pl.pallas_call(
        _prep_d_kernel,
        grid=(m // tm,),
        in_specs=[row(CD_W, 4), row(CD_W, 5), row(N_ALIGN, CD_FD // N_ALIGN), par, par, par],
        out_specs=[full(CD_W), full(CD_W), full(CD_W), full(LANES)],
        out_shape=[sds(CD_W, BF16), sds(CD_W, F32), sds(CD_W, BF16), sds(LANES, F32)],
        compiler_params=_cparams(("parallel",)),
        name="prep_d",
    )(za, za, zb, gq, gk, fb)


def _fox_kernel(q_ref, k_ref, v_ref, cq_ref, ck_ref, g_ref, o_ref, *, tq):
    t = q_ref.shape[0]
    dn = (((1,), (1,)), ((), ()))
    row = lax.broadcasted_iota(jnp.int32, (tq, tq), 0)
    col = lax.broadcasted_iota(jnp.int32, (tq, tq), 1)
    for qi in range(t // tq):
        lo, hi = qi * tq, (qi + 1) * tq
        q = q_ref[lo:hi, :]
        cq = cq_ref[lo:hi, :]
        s_d = lax.dot_general(q, k_ref[lo:hi, :], dn, preferred_element_type=F32) + cq - ck_ref[:, lo:hi]
        s_d = jnp.where(col <= row, s_d, -jnp.inf)
        m = jnp.max(s_d, axis=-1, keepdims=True)
        if qi:
            s_o = lax.dot_general(q, k_ref[0:lo, :], dn, preferred_element_type=F32) + cq - ck_ref[:, 0:lo]
            m = jnp.maximum(m, jnp.max(s_o, axis=-1, keepdims=True))
        p_d = jnp.exp(s_d - m)
        l = jnp.sum(p_d, axis=-1, keepdims=True)
        acc = jnp.dot(p_d.astype(BF16), v_ref[lo:hi, :].astype(BF16), preferred_element_type=F32)
        if qi:
            p_o = jnp.exp(s_o - m)
            l = l + jnp.sum(p_o, axis=-1, keepdims=True)
            acc = acc + jnp.dot(p_o.astype(BF16), v_ref[0:lo, :].astype(BF16), preferred_element_type=F32)
        o = acc / l * jax.nn.sigmoid(g_ref[lo:hi, :])
        o_ref[lo:hi, :] = o.astype(o_ref.dtype)


def _fox_prefill(q, k, za, zb, cum_col, cum_row, *, n_batch, t):
    hb = CD_W // HEAD_DIM
    tq = min(FOX_TQ, t)
    qmap = lambda bi, h: (bi, h)
    return pl.pallas_call(
        functools.partial(_fox_kernel, tq=tq),
        grid=(n_batch, D_HEADS),
        in_specs=[pl.BlockSpec((t, HEAD_DIM), qmap),
                  pl.BlockSpec((t, HEAD_DIM), qmap),
                  pl.BlockSpec((t, HEAD_DIM), lambda bi, h: (bi, 6 * hb + h)),
                  pl.BlockSpec((None, None, t, 1), lambda bi, h: (bi, h, 0, 0)),
                  pl.BlockSpec((None, None, 1, t), lambda bi, h: (bi, h, 0, 0)),
                  pl.BlockSpec((t, HEAD_DIM), lambda bi, h: (bi, h))],
        out_specs=pl.BlockSpec((t, HEAD_DIM), qmap),
        out_shape=jax.ShapeDtypeStruct((n_batch * t, CD_W), BF16),
        compiler_params=_cparams(("parallel", "parallel")),
        name="fox_prefill",
    )(q, k, za, cum_col, cum_row, zb)


def _fox_dec_kernel(q_ref, kc_ref, vc_ref, kn_ref, vn_ref, cq_ref, ckc_ref, ckn_ref, g_ref, o_ref):
    t = q_ref.shape[0]
    q = q_ref[...]
    dn = (((1,), (1,)), ((), ()))
    s_c = lax.dot_general(q, kc_ref[...].astype(BF16), dn, preferred_element_type=F32)
    s_c = s_c + cq_ref[...] - ckc_ref[...]
    s_n = lax.dot_general(q, kn_ref[...], dn, preferred_element_type=F32)
    s_n = s_n + cq_ref[...] - ckn_ref[...]
    row = lax.broadcasted_iota(jnp.int32, (t, t), 0)
    col = lax.broadcasted_iota(jnp.int32, (t, t), 1)
    s_n = jnp.where(col <= row, s_n, -jnp.inf)
    m = jnp.maximum(jnp.max(s_c, axis=-1, keepdims=True), jnp.max(s_n, axis=-1, keepdims=True))
    p_c = jnp.exp(s_c - m)
    p_n = jnp.exp(s_n - m)
    l = jnp.sum(p_c, axis=-1, keepdims=True) + jnp.sum(p_n, axis=-1, keepdims=True)
    acc = jnp.dot(p_c.astype(BF16), vc_ref[...].astype(BF16), preferred_element_type=F32)
    acc = acc + jnp.dot(p_n.astype(BF16), vn_ref[...].astype(BF16), preferred_element_type=F32)
    o_ref[...] = (acc / l * jax.nn.sigmoid(g_ref[...])).astype(o_ref.dtype)


def _fox_decode(q, k, za, zb, k_cache, v_cache, cq_col, ck_cache_row, ck_new_row, *, n_batch, t, row0):
    p = k_cache.shape[1]
    hb = CD_W // HEAD_DIM
    r0 = row0 // t
    new = lambda seg: pl.BlockSpec((t, HEAD_DIM), lambda bi, h: (r0 + bi, seg * hb + h))
    cache = pl.BlockSpec((None, p, HEAD_DIM), lambda bi, h: (bi, 0, h))
    return pl.pallas_call(
        _fox_dec_kernel,
        grid=(n_batch, D_HEADS),
        in_specs=[new(0), cache, cache, new(0), new(6),
                  pl.BlockSpec((None, None, t, 1), lambda bi, h: (bi, h, 0, 0)),
                  pl.BlockSpec((None, None, 1, p), lambda bi, h: (bi, h, 0, 0)),
                  pl.BlockSpec((None, None, 1, t), lambda bi, h: (bi, h, 0, 0)),
                  new(0)],
        out_specs=pl.BlockSpec((t, HEAD_DIM), lambda bi, h: (bi, h)),
        out_shape=jax.ShapeDtypeStruct((n_batch * t, CD_W), BF16),
        compiler_params=_cparams(("parallel", "parallel")),
        name="fox_decode",
    )(q, k_cache, v_cache, k, za, cq_col, ck_cache_row, ck_new_row, zb)


def _pad_cols(w, n):
    return jnp.pad(w, ((0, 0), (0, n - w.shape[1])))


def _ab_tail_weight(w):
    o = np.cumsum((0,) + AB_SIZES)
    return jnp.concatenate([w[:, o[6]:], _pad_cols(w[:, o[4]:o[6]], N_ALIGN)], axis=1).astype(BF16)


def _cd_tail_weight(w):
    o = np.cumsum((0,) + CD_SIZES)
    return jnp.concatenate([w[:, o[8]:], _pad_cols(w[:, o[7]:o[8]], N_ALIGN)], axis=1).astype(BF16)


def _expand_groups(mod_p, mod_s, seq_p):
    return jnp.concatenate([jnp.repeat(mod_p, seq_p // ROW_GROUP, axis=0), mod_s], axis=0)


def _lane_row(v):
    return jnp.pad(v.astype(F32), (0, LANES - v.shape[0]))[None, :]


def kernel(x_prompt, x_sample, c_prompt, c_sample, cache_a_k, cache_a_v, cache_a_idx_k, state_b_conv, state_c_s, cache_d_k, cache_d_v, cache_d_logf, ada_w, ada_b, norm_mix, norm_ffn, ab_w_in, ab_w_out, a_q_norm, a_k_norm, b_conv_w, cd_w_in, cd_w_out, c_lb, c_o_norm, d_q_norm, d_k_norm, d_f_bias, ffn_w_gate, ffn_w_up, ffn_w_down):
    bp, tp, d = x_prompt.shape
    bs, ts, _ = x_sample.shape
    past = cache_a_k.shape[2]
    depth = ada_w.shape[0]
    assert ts == ROW_GROUP and tp % ROW_GROUP == 0 and d == D_MODEL
    mp, ms = bp * tp, bs * ts

    x = jnp.concatenate([x_prompt.reshape(mp, d), x_sample.reshape(ms, d)], axis=0)
    c_all = jnp.concatenate([c_prompt, c_sample], axis=0)
    c_rows = -(-c_all.shape[0] // SUBLANES) * SUBLANES
    c_pad = jnp.pad(c_all, ((0, c_rows - c_all.shape[0]), (0, 0)))

    lb_all = jnp.cumsum(jax.nn.softmax(c_lb.astype(F32), axis=0), axis=0)
    lb_all = lb_all - lb_all[0]

    pos = jnp.concatenate([jnp.tile(jnp.arange(tp), bp), past + jnp.tile(jnp.arange(ts), bs)])
    cos_q, sin_q = _rope_tables(pos, HEAD_DIM)
    cos_i, sin_i = _rope_tables(pos, IDX_DIM)

    outs_p, outs_s = {}, {}
    mod_all = _ada_mod(c_pad, ada_w, ada_b[:, None, :])
    for l in range(depth):
        mods = [_expand_groups(m[:bp], m[bp:bp + bs], tp) for m in jnp.split(mod_all[l], 6, axis=-1)]
        sh1, sc1, g1, sh2, sc2, g2 = mods
        h = _norm_mod(x, norm_mix[l][None, :], sc1, sh1)
        if l % 2 == 0:
            e = l // 2
            za = _matmul_head(h, ab_w_in, e, AB_HEAD)
            z = _matmul(h, _ab_tail_weight(ab_w_in[e]))
            q_b, qi_b, k_f, k_b, v_b, ki_f, ki_lo, ki_hi = _prep_a(
                za, z, cos_q, sin_q, cos_i, sin_i, a_q_norm[e][None, :], a_k_norm[e][None, :])
            v_f = za[:, AB_V:AB_V + A_KV_HEADS * HEAD_DIM]

            keys_p = [a[:mp].reshape(bp, tp, -1) for a in (ki_lo, ki_hi, k_b, v_b)]
            grp = DSA_GROUP if tp % DSA_GROUP == 0 else tp
            ya_groups = []
            for g0 in range(0, tp, grp):
                o = _dsa_attention(qi_b, keys_p[0], keys_p[1], z, q_b, keys_p[2], keys_p[3], n_batch=bp, t=grp,
                                   tq=min(Q_BLOCK, grp), row0=g0, q_stride=tp, lp=g0 + grp, q_pos0=g0,
                                   l_valid=g0 + grp, n_sel=min(TOPK_MAX, tp // 4))
                ya_groups.append(o.reshape(bp, grp, -1))
            ya_p = jnp.concatenate(ya_groups, axis=1).reshape(mp, -1)
            ls = past + ts
            lp = -(-ls // LANES) * LANES

            def with_cache(cache, new_rows):
                cache = cache.reshape(bs, past, -1).astype(BF16)
                new_rows = new_rows[mp:].reshape(bs, ts, -1)
                full = jnp.concatenate([cache, new_rows], axis=1)
                return jnp.pad(full, ((0, 0), (0, lp - ls), (0, 0)))

            zi = jnp.zeros_like(cache_a_idx_k[e])
            ya_s = _dsa_attention(
                qi_b, with_cache(jnp.concatenate([cache_a_idx_k[e], zi], -1), ki_lo),
                with_cache(jnp.concatenate([zi, cache_a_idx_k[e]], -1), ki_hi), z, q_b,
                with_cache(cache_a_k[e], k_b), with_cache(cache_a_v[e], v_b),
                n_batch=bs, t=ts, tq=ts, row0=mp, q_stride=ts, lp=lp, q_pos0=past, l_valid=ls,
                n_sel=min(TOPK_MAX, ls // 4))

            yb_p, conv_p = _short_conv(z, jnp.zeros((bp, CONV_W - 1, B_WIDTH), F32), b_conv_w[e],
                                       n_batch=bp, t=tp, row0=0)
            yb_s, conv_s = _short_conv(z, state_b_conv[e], b_conv_w[e], n_batch=bs, t=ts, row0=mp)
            x = _mix_out(jnp.concatenate([ya_p, ya_s], axis=0), jnp.concatenate([yb_p, yb_s], axis=0),
                         ab_w_out[e], x, g1)

            for od, sl, bn, t, conv_new in ((outs_p, slice(0, mp), bp, tp, conv_p),
                                            (outs_s, slice(mp, mp + ms), bs, ts, conv_s)):
                od.setdefault("a_k", []).append(k_f[sl].reshape(bn, t, A_KV_HEADS, HEAD_DIM))
                od.setdefault("a_v", []).append(v_f[sl].reshape(bn, t, A_KV_HEADS, HEAD_DIM))
                od.setdefault("a_ik", []).append(ki_f[sl].reshape(bn, t, IDX_DIM))
                od.setdefault("b_conv", []).append(conv_new)
        else:
            od_ = l // 2
            za = _matmul_head(h, cd_w_in, od_, CD_HEAD)
            z = _matmul(h, _cd_tail_weight(cd_w_in[od_]))
            q_b, k_f, k_b, lf_slab = _prep_d(za, z, d_q_norm[od_][None, :], d_k_norm[od_][None, :],
                                             _lane_row(d_f_bias[od_]))
            logf = lf_slab[:, :D_HEADS]
            v_f = za[:, 6 * CD_W:7 * CD_W]
            lb = lb_all[l][None, :]
            gain = c_o_norm[od_][None, :]

            yc_p, s_p = _hgrn2(za, lb, gain, jnp.zeros((bp, C_HEADS, C_DK, C_DV), F32), n_batch=bp, t=tp, row0=0)
            yc_s, s_s = _hgrn2(za, lb, gain, state_c_s[od_], n_batch=bs, t=ts, row0=mp)

            cum_p = jnp.cumsum(logf[:mp].reshape(bp, tp, D_HEADS), axis=1).transpose(0, 2, 1)
            yd_p = _fox_prefill(q_b, k_b, za, z, cum_p[..., None], cum_p[:, :, None, :], n_batch=bp, t=tp)
            cum_s = jnp.cumsum(jnp.concatenate([cache_d_logf[od_], logf[mp:].reshape(bs, ts, D_HEADS)], axis=1),
                               axis=1).transpose(0, 2, 1)
            yd_s = _fox_decode(q_b, k_b, za, z, cache_d_k[od_].reshape(bs, past, CD_W),
                               cache_d_v[od_].reshape(bs, past, CD_W), cum_s[:, :, past:, None],
                               cum_s[:, :, None, :past], cum_s[:, :, None, past:], n_batch=bs, t=ts, row0=mp)
            x = _mix_out(jnp.concatenate([yc_p, yc_s], axis=0), jnp.concatenate([yd_p, yd_s], axis=0),
                         cd_w_out[od_], x, g1)

            for od, sl, bn, t, s_new in ((outs_p, slice(0, mp), bp, tp, s_p),
                                         (outs_s, slice(mp, mp + ms), bs, ts, s_s)):
                od.setdefault("c_s", []).append(s_new)
                od.setdefault("d_k", []).append(k_f[sl].reshape(bn, t, D_HEADS, HEAD_DIM))
                od.setdefault("d_v", []).append(v_f[sl].reshape(bn, t, D_HEADS, HEAD_DIM))
                od.setdefault("d_lf", []).append(logf[sl].reshape(bn, t, D_HEADS))
        h2 = _norm_mod(x, norm_ffn[l][None, :], sc2, sh2)
        hid = _ffn_gate_up(h2, ffn_w_gate, ffn_w_up, l)
        x = _ffn_down(hid, ffn_w_down[l].astype(BF16), x, g2)

    y_prompt = x[:mp].reshape(bp, tp, d)
    y_sample = x[mp:].reshape(bs, ts, d)
    names = ("a_k", "a_v", "a_ik", "b_conv", "c_s", "d_k", "d_v", "d_lf")
    return ((y_prompt, y_sample) + tuple(jnp.stack(outs_p[n]) for n in names)
            + tuple(jnp.stack(outs_s[n]) for n in names))
```

```python
import functools

import numpy as np
import jax
import jax.numpy as jnp
from jax import lax
from jax.experimental import pallas as pl
from jax.experimental.pallas import tpu as pltpu

F32 = jnp.float32
BF16 = jnp.bfloat16

D_MODEL = 4096
CHUNK = 64
Q_BLOCK = 128
HEAD_DIM = 128
ROPE_THETA = 500000.0
ROT_FRACTION = 4
NORM_EPS = 1e-6
A_HEADS = D_MODEL // 256
A_KV_HEADS = A_HEADS // 4
IDX_HEADS = D_MODEL // 128
IDX_DIM = 64
TOPK_MAX = 256
B_WIDTH = D_MODEL // 2
CONV_W = 3
C_HEADS = D_MODEL // 256
C_DK = 128
C_DV = 128
D_HEADS = D_MODEL // 256
AB_SIZES = (A_HEADS * HEAD_DIM, A_KV_HEADS * HEAD_DIM, A_KV_HEADS * HEAD_DIM,
            IDX_HEADS * IDX_DIM, IDX_DIM, IDX_HEADS, B_WIDTH, B_WIDTH, B_WIDTH)
CD_SIZES = (C_HEADS * C_DK, C_HEADS * C_DK, C_HEADS * C_DV, C_HEADS * C_DV,
            D_HEADS * HEAD_DIM, D_HEADS * HEAD_DIM, D_HEADS * HEAD_DIM, D_HEADS, D_HEADS * HEAD_DIM)

LANES = 128
SUBLANES = 8
VMEM_LIMIT_BYTES = 56 * 1024 * 1024

ROW_GROUP = 16
N_ALIGN = 512
INT_MIN = -2 ** 31
HGRN_UNROLL = 4
FOX_TQ = 256
DSA_GROUP = 512

AB_Q, AB_K, AB_V, AB_QI, AB_HEAD = 0, 2048, 2560, 3072, 5120
AB_BG, AB_CG, AB_BIN, AB_KW = 0, 2048, 4096, 6144
CD_W = 2048
CD_HEAD = 7 * CD_W
CD_FD = CD_W


def _cparams(sem):
    return pltpu.CompilerParams(dimension_semantics=sem, vmem_limit_bytes=VMEM_LIMIT_BYTES)


def _pick_tile(n, candidates):
    for c in candidates:
        if n % c == 0:
            return c
    raise ValueError(f"no tile for {n}")


M_TILES = (768, 512, 256, 128, 96, 64, 32, 16)


def _ada_kernel(c_ref, w_ref, b_ref, o_ref):
    c = c_ref[...]
    a = (c * jax.nn.sigmoid(c)).astype(BF16)
    o_ref[...] = jnp.dot(a, w_ref[...].astype(BF16), preferred_element_type=F32) + b_ref[...]


def _ada_mod(c, w, b):
    r, d = c.shape
    depth, _, n = w.shape
    tn = _pick_tile(n, (512, 256, 128))
    return pl.pallas_call(
        _ada_kernel,
        grid=(depth, n // tn),
        in_specs=[pl.BlockSpec((r, d), lambda l, j: (0, 0)),
                  pl.BlockSpec((None, d, tn), lambda l, j: (l, 0, j)),
                  pl.BlockSpec((None, 1, tn), lambda l, j: (l, 0, j))],
        out_specs=pl.BlockSpec((None, r, tn), lambda l, j: (l, 0, j)),
        out_shape=jax.ShapeDtypeStruct((depth, r, n), F32),
        compiler_params=_cparams(("parallel", "parallel")),
        name="ada_mod",
    )(c, w, b)


def _norm_mod_kernel(x_ref, gain_ref, sc_ref, sh_ref, o_ref, *, groups):
    gain = gain_ref[...]
    for g in range(groups):
        rows = slice(g * ROW_GROUP, (g + 1) * ROW_GROUP)
        x = x_ref[rows, :]
        ms = jnp.mean(x * x, axis=-1, keepdims=True)
        y = x * lax.rsqrt(ms + NORM_EPS) * gain
        o_ref[rows, :] = (y * (1.0 + sc_ref[g:g + 1, :]) + sh_ref[g:g + 1, :]).astype(o_ref.dtype)


def _norm_mod(x, gain, g16, sc_seg, sh_seg):
    m, d = x.shape
    tm = _pick_tile(m, (384,) + M_TILES)
    groups = tm // ROW_GROUP
    return pl.pallas_call(
        functools.partial(_norm_mod_kernel, groups=groups),
        grid=(m // tm,),
        in_specs=[pl.BlockSpec((tm, d), lambda i: (i, 0)),
                  pl.BlockSpec((1, d), lambda i: (0, 0)),
                  pl.BlockSpec((groups, d), lambda i: (i, sc_seg)),
                  pl.BlockSpec((groups, d), lambda i: (i, sh_seg))],
        out_specs=pl.BlockSpec((tm, d), lambda i: (i, 0)),
        out_shape=jax.ShapeDtypeStruct((m, d), BF16),
        compiler_params=_cparams(("parallel",)),
        name="norm_mod",
    )(x, gain, g16, g16)


def _mm_kernel(a_ref, w_ref, o_ref):
    o_ref[...] = jnp.dot(a_ref[...], w_ref[...], preferred_element_type=F32).astype(o_ref.dtype)


def _matmul(a, w):
    m, k = a.shape
    n = w.shape[1]
    tm = _pick_tile(m, M_TILES)
    tn = _pick_tile(n, (512, 256, 128))
    return pl.pallas_call(
        _mm_kernel,
        grid=(m // tm, n // tn),
        in_specs=[pl.BlockSpec((tm, k), lambda i, j: (i, 0)),
                  pl.BlockSpec((k, tn), lambda i, j: (0, j))],
        out_specs=pl.BlockSpec((tm, tn), lambda i, j: (i, j)),
        out_shape=jax.ShapeDtypeStruct((m, n), F32),
        compiler_params=_cparams(("parallel", "parallel")),
        name="proj_in",
    )(a, w)


def _mm_w32_kernel(a_ref, w_ref, o_ref):
    o_ref[...] = jnp.dot(a_ref[...], w_ref[...].astype(BF16), preferred_element_type=F32)


def _matmul_head(a, w, layer, n_cols):
    m, k = a.shape
    tm = _pick_tile(m, (1056,) + M_TILES)
    tn = _pick_tile(n_cols, (512, 256, 128))
    return pl.pallas_call(
        _mm_w32_kernel,
        grid=(m // tm, n_cols // tn),
        in_specs=[pl.BlockSpec((tm, k), lambda i, j: (i, 0)),
                  pl.BlockSpec((None, k, tn), lambda i, j: (layer, 0, j))],
        out_specs=pl.BlockSpec((tm, tn), lambda i, j: (i, j)),
        out_shape=jax.ShapeDtypeStruct((m, n_cols), F32),
        compiler_params=_cparams(("parallel", "parallel")),
        name="proj_in_head",
    )(a, w)


def _gated_residual(o_ref, x_ref, g_ref, acc, groups):
    for g in range(groups):
        rows = slice(g * ROW_GROUP, (g + 1) * ROW_GROUP)
        o_ref[rows, :] = x_ref[rows, :] + g_ref[g:g + 1, :] * acc[rows, :]


def _mix_out_kernel(a1_ref, a2_ref, w1_ref, w2_ref, x_ref, g_ref, o_ref, *, groups):
    acc = jnp.dot(a1_ref[...], w1_ref[...].astype(BF16), preferred_element_type=F32)
    acc = acc + jnp.dot(a2_ref[...], w2_ref[...].astype(BF16), preferred_element_type=F32)
    _gated_residual(o_ref, x_ref, g_ref, acc, groups)


def _mix_out(a1, a2, w, x, g16, gate_seg):
    m, kh = a1.shape
    n = w.shape[1]
    assert w.shape[0] == 2 * kh
    tm = _pick_tile(m, M_TILES)
    tn = _pick_tile(n, (512, 256, 128))
    groups = tm // ROW_GROUP
    return pl.pallas_call(
        functools.partial(_mix_out_kernel, groups=groups),
        grid=(m // tm, n // tn),
        in_specs=[pl.BlockSpec((tm, kh), lambda i, j: (i, 0)),
                  pl.BlockSpec((tm, kh), lambda i, j: (i, 0)),
                  pl.BlockSpec((kh, tn), lambda i, j: (0, j)),
                  pl.BlockSpec((kh, tn), lambda i, j: (1, j)),
                  pl.BlockSpec((tm, tn), lambda i, j: (i, j)),
                  pl.BlockSpec((groups, tn), lambda i, j: (i, gate_seg * (n // tn) + j))],
        out_specs=pl.BlockSpec((tm, tn), lambda i, j: (i, j)),
        out_shape=jax.ShapeDtypeStruct((m, n), F32),
        compiler_params=_cparams(("parallel", "parallel")),
        name="mix_out_resid",
    )(a1, a2, w, w, x, g16)


def _ffn_down_kernel(a_ref, w_ref, x_ref, g_ref, o_ref, acc_ref, *, nk, groups):
    k = pl.program_id(2)
    part = jnp.dot(a_ref[...], w_ref[...], preferred_element_type=F32)
    if nk == 1:
        _gated_residual(o_ref, x_ref, g_ref, part, groups)
    else:
        @pl.when(k == 0)
        def _():
            acc_ref[...] = part

        @pl.when(jnp.logical_and(k > 0, k < nk - 1))
        def _():
            acc_ref[...] += part

        @pl.when(k == nk - 1)
        def _():
            _gated_residual(o_ref, x_ref, g_ref, acc_ref[...] + part, groups)


def _ffn_down(a, w, layer, x, g16, gate_seg):
    m, k = a.shape
    n = w.shape[2]
    tm = _pick_tile(m, M_TILES)
    tn = _pick_tile(n, (512, 256, 128))
    tk = k if k <= 4096 else _pick_tile(k, (5504, 4096, 2048, 1024))
    nk = k // tk
    groups = tm // ROW_GROUP
    return pl.pallas_call(
        functools.partial(_ffn_down_kernel, nk=nk, groups=groups),
        grid=(m // tm, n // tn, nk),
        in_specs=[pl.BlockSpec((tm, tk), lambda i, j, kk: (i, kk)),
                  pl.BlockSpec((None, tk, tn), lambda i, j, kk: (layer, kk, j)),
                  pl.BlockSpec((tm, tn), lambda i, j, kk: (i, j)),
                  pl.BlockSpec((groups, tn), lambda i, j, kk: (i, gate_seg * (n // tn) + j))],
        out_specs=pl.BlockSpec((tm, tn), lambda i, j, kk: (i, j)),
        out_shape=jax.ShapeDtypeStruct((m, n), F32),
        scratch_shapes=[pltpu.VMEM((tm, tn), F32)],
        compiler_params=_cparams(("parallel", "parallel", "arbitrary")),
        name="ffn_down_resid",
    )(a, w, x, g16)


def _ffn_gate_up_kernel(a_ref, wg_ref, wu_ref, o_ref):
    a = a_ref[...]
    g = jnp.dot(a, wg_ref[...].astype(BF16), preferred_element_type=F32)
    u = jnp.dot(a, wu_ref[...].astype(BF16), preferred_element_type=F32)
    o_ref[...] = (g * jax.nn.sigmoid(g) * u).astype(o_ref.dtype)


def _ffn_gate_up(a, wg, wu, layer):
    m, k = a.shape
    n = wg.shape[2]
    tm = _pick_tile(m, (1056,) + M_TILES)
    tn = _pick_tile(n, (256, 128))
    return pl.pallas_call(
        _ffn_gate_up_kernel,
        grid=(m // tm, n // tn),
        in_specs=[pl.BlockSpec((tm, k), lambda i, j: (i, 0)),
                  pl.BlockSpec((None, k, tn), lambda i, j: (layer, 0, j)),
                  pl.BlockSpec((None, k, tn), lambda i, j: (layer, 0, j))],
        out_specs=pl.BlockSpec((tm, tn), lambda i, j: (i, j)),
        out_shape=jax.ShapeDtypeStruct((m, n), BF16),
        compiler_params=_cparams(("parallel", "parallel")),
        name="ffn_gate_up",
    )(a, wg, wu)


def _rope_lanes(y, cos, sin_signed, half, period):
    lane = lax.broadcasted_iota(jnp.int32, y.shape, 1)
    first = jnp.bitwise_and(lane, period - 1) < half
    swap = jnp.where(first, pltpu.roll(y, LANES - half, 1), pltpu.roll(y, half, 1))
    return y * cos + swap * sin_signed


def _head_rms(x, gain):
    ms = jnp.mean(x * x, axis=-1, keepdims=True)
    return x * lax.rsqrt(ms + NORM_EPS) * gain


def _prep_a_kernel(zq_ref, zqi0_ref, zqi1_ref, zkv_ref, zkw_ref, cq_ref, sq_ref, ci_ref, si_ref, gq_ref, gk_ref,
                   q_ref, qi_ref, k_ref, kb_ref, vb_ref, ki_ref, kilo_ref, kihi_ref):
    cq, sq, ci, si = cq_ref[...], sq_ref[...], ci_ref[...], si_ref[...]
    half_q = HEAD_DIM // ROT_FRACTION // 2
    half_i = IDX_DIM // ROT_FRACTION // 2
    for h in range(A_HEADS):
        cols = slice(h * HEAD_DIM, (h + 1) * HEAD_DIM)
        y = _rope_lanes(_head_rms(zq_ref[:, cols], gq_ref[...]), cq, sq, half_q, HEAD_DIM)
        q_ref[:, cols] = (y * (HEAD_DIM ** -0.5)).astype(BF16)
    for h in range(A_KV_HEADS):
        cols = slice(h * HEAD_DIM, (h + 1) * HEAD_DIM)
        y = _rope_lanes(_head_rms(zkv_ref[:, cols], gk_ref[...]), cq, sq, half_q, HEAD_DIM)
        k_ref[:, cols] = y
        kb_ref[:, cols] = y.astype(BF16)
    nkv = A_KV_HEADS * HEAD_DIM
    vb_ref[...] = zkv_ref[:, nkv:2 * nkv].astype(BF16)
    slabs = zqi0_ref.shape[1] // LANES
    for j in range(IDX_HEADS * IDX_DIM // LANES):
        src = zqi0_ref if j < slabs else zqi1_ref
        x = src[:, (j % slabs) * LANES:(j % slabs + 1) * LANES]
        qi_ref[:, j * LANES:(j + 1) * LANES] = _rope_lanes(x, ci, si, half_i, IDX_DIM).astype(BF16)
    slab = _rope_lanes(zkw_ref[:, 0:LANES], ci, si, half_i, IDX_DIM)
    ki_ref[...] = slab[:, 0:IDX_DIM]
    lane = lax.broadcasted_iota(jnp.int32, slab.shape, 1)
    lo = jnp.where(lane < IDX_DIM, slab, 0.0)
    kilo_ref[...] = lo.astype(BF16)
    kihi_ref[...] = pltpu.roll(lo, IDX_DIM, 1).astype(BF16)


def _prep_a(za, zb, cq, sq, ci, si, gq, gk):
    m = za.shape[0]
    tm = _pick_tile(m, (256, 128, 96, 64, 32, 16))
    row = lambda w, c0: pl.BlockSpec((tm, w), lambda i: (i, c0 // w))
    full = lambda w: pl.BlockSpec((tm, w), lambda i: (i, 0))
    par = pl.BlockSpec((1, LANES), lambda i: (0, 0))
    nq, nkv = A_HEADS * HEAD_DIM, A_KV_HEADS * HEAD_DIM
    nqi = IDX_HEADS * IDX_DIM // 2
    sds = lambda w, dt: jax.ShapeDtypeStruct((m, w), dt)
    return pl.pallas_call(
        _prep_a_kernel,
        grid=(m // tm,),
        in_specs=[row(nq, AB_Q), row(nqi, AB_QI), row(nqi, AB_QI + nqi), row(2 * nkv, AB_K),
                  row(N_ALIGN, AB_KW), full(LANES), full(LANES), full(LANES), full(LANES), par, par],
        out_specs=[full(nq), full(nq), full(nkv), full(nkv), full(nkv), full(IDX_DIM), full(LANES), full(LANES)],
        out_shape=[sds(nq, BF16), sds(nq, BF16), sds(nkv, F32), sds(nkv, BF16), sds(nkv, BF16),
                   sds(IDX_DIM, F32), sds(LANES, BF16), sds(LANES, BF16)],
        compiler_params=_cparams(("parallel",)),
        name="prep_a",
    )(za, za, za, za, zb, cq, sq, ci, si, gq, gk)


def _rope_tables(pos, head_dim):
    r = head_dim // ROT_FRACTION
    half = r // 2
    inv = 1.0 / (ROPE_THETA ** (jnp.arange(half, dtype=F32) / half))
    ang = pos.astype(F32)[:, None] * inv[None, :]
    cos, sin = jnp.cos(ang), jnp.sin(ang)
    lane = np.arange(LANES) % head_dim
    idx = lane % half
    in_rot = jnp.asarray(lane < r)[None, :]
    first = jnp.asarray(lane < half)[None, :]
    c = jnp.where(in_rot, cos[:, idx], 1.0)
    s = jnp.where(first, -sin[:, idx], jnp.where(in_rot, sin[:, idx], 0.0))
    return c, s


def _dsa_kernel(qi_ref, kilo_ref, kihi_ref, wi_ref, q_ref, k_ref, v_ref, o_ref, key_scr, bias_scr, *,
                q_pos0, l_valid, n_sel):
    tq = q_ref.shape[0]
    lp = k_ref.shape[0]
    dn = (((1,), (1,)), ((), ()))

    kilo = kilo_ref[...]
    kihi = kihi_ref[...]
    wi = wi_ref[...] * ((IDX_HEADS * IDX_DIM) ** -0.5)
    score = jnp.zeros((tq, lp), F32)
    for j in range(IDX_HEADS // 2):
        qi2 = qi_ref[:, j * LANES:(j + 1) * LANES]
        s_lo = lax.dot_general(qi2, kilo, dn, preferred_element_type=F32)
        s_hi = lax.dot_general(qi2, kihi, dn, preferred_element_type=F32)
        c = IDX_DIM + 2 * j
        score = score + jnp.maximum(s_lo, 0.0) * wi[:, c:c + 1]
        score = score + jnp.maximum(s_hi, 0.0) * wi[:, c + 1:c + 2]

    q_pos = q_pos0 + pl.program_id(1) * tq + lax.broadcasted_iota(jnp.int32, (tq, lp), 0)
    k_pos = lax.broadcasted_iota(jnp.int32, (tq, lp), 1)
    shift = CHUNK.bit_length() - 1
    adm = jnp.logical_and(jnp.right_shift(k_pos, shift) <= jnp.right_shift(q_pos, shift), k_pos < l_valid)
    bits = pltpu.bitcast(score, jnp.int32)
    key = jnp.where(bits < 0, jnp.bitwise_xor(bits, jnp.int32(0x7FFFFFFF)), bits)
    key_scr[...] = jnp.where(adm, key, jnp.int32(INT_MIN))

    def count_ge(t):
        return jnp.sum(jnp.where(key_scr[...] >= t, 1.0, 0.0), axis=-1, keepdims=True)

    zero = jnp.zeros((tq, 1), jnp.int32)
    t0 = jnp.where(count_ge(zero) >= n_sel, zero, jnp.int32(INT_MIN))

    def body(i, t):
        cand = jnp.bitwise_or(t, jnp.left_shift(jnp.int32(1), 30 - i))
        return jnp.where(count_ge(cand) >= n_sel, cand, t)

    thr = lax.fori_loop(0, 31, body, t0)
    keys = key_scr[...]
    bias_scr[...] = jnp.where(jnp.logical_and(keys >= thr, adm), 0.0, -jnp.inf)

    need = n_sel - jnp.sum(jnp.where(keys > thr, 1.0, 0.0), axis=-1, keepdims=True)
    n_tied = jnp.sum(jnp.where(keys == thr, 1.0, 0.0), axis=-1, keepdims=True)
    finite_thr = thr > jnp.int32(INT_MIN)
    over = jnp.logical_and(n_tied > need, finite_thr)

    @pl.when(jnp.max(jnp.where(over, 1.0, 0.0)) > 0.0)
    def _():
        tri = jnp.where(lax.broadcasted_iota(jnp.int32, (LANES, LANES), 0)
                        <= lax.broadcasted_iota(jnp.int32, (LANES, LANES), 1), 1.0, 0.0).astype(BF16)
        carry = jnp.zeros((tq, 1), F32)
        for c in range(lp // LANES):
            cols = slice(c * LANES, (c + 1) * LANES)
            k_c = key_scr[:, cols]
            tied = jnp.where(k_c == thr, 1.0, 0.0)
            rank = jnp.dot(tied.astype(BF16), tri, preferred_element_type=F32) + carry
            keep_tie = jnp.logical_and(jnp.logical_and(k_c == thr, rank <= need), finite_thr)
            bias_scr[:, cols] = jnp.where(jnp.logical_or(k_c > thr, keep_tie), 0.0, -jnp.inf)
            carry = carry + jnp.sum(tied, axis=-1, keepdims=True)

    group = A_HEADS // A_KV_HEADS
    for kvh in range(A_KV_HEADS):
        kk = k_ref[:, kvh * HEAD_DIM:(kvh + 1) * HEAD_DIM]
        vv = v_ref[:, kvh * HEAD_DIM:(kvh + 1) * HEAD_DIM]
        for g in range(group):
            cols = slice((kvh * group + g) * HEAD_DIM, (kvh * group + g + 1) * HEAD_DIM)
            s = lax.dot_general(q_ref[:, cols], kk, dn, preferred_element_type=F32) + bias_scr[...]
            m = jnp.max(s, axis=-1, keepdims=True)
            p = jnp.exp(s - m)
            l = jnp.sum(p, axis=-1, keepdims=True)
            o = jnp.dot(p.astype(BF16), vv, preferred_element_type=F32) / l
            o_ref[:, cols] = o.astype(o_ref.dtype)


def _dsa_attention(qi, ki_lo, ki_hi, z, q, k, v, *, n_batch, t, tq, row0, q_stride, lp, q_pos0, l_valid, n_sel):
    nq = t // tq
    qrow = lambda bi, i: (row0 + bi * q_stride) // tq + i
    qmap = lambda bi, i: (qrow(bi, i), 0)
    kmap = lambda bi, i: (bi, 0, 0)
    return pl.pallas_call(
        functools.partial(_dsa_kernel, q_pos0=q_pos0, l_valid=l_valid, n_sel=n_sel),
        grid=(n_batch, nq),
        in_specs=[pl.BlockSpec((tq, qi.shape[1]), qmap),
                  pl.BlockSpec((None, lp, LANES), kmap),
                  pl.BlockSpec((None, lp, LANES), kmap),
                  pl.BlockSpec((tq, LANES), lambda bi, i: (qrow(bi, i), AB_KW // LANES)),
                  pl.BlockSpec((tq, q.shape[1]), qmap),
                  pl.BlockSpec((None, lp, k.shape[2]), kmap),
                  pl.BlockSpec((None, lp, v.shape[2]), kmap)],
        out_specs=pl.BlockSpec((tq, q.shape[1]), lambda bi, i: (bi * nq + i, 0)),
        out_shape=jax.ShapeDtypeStruct((n_batch * t, q.shape[1]), BF16),
        scratch_shapes=[pltpu.VMEM((tq, lp), jnp.int32), pltpu.VMEM((tq, lp), F32)],
        compiler_params=_cparams(("parallel", "arbitrary")),
        name="dsa_attention",
    )(qi, ki_lo, ki_hi, z, q, k, v)


def _conv_kernel(bg_ref, cg_ref, u_ref, st_ref, w_ref, y_ref, ns_ref):
    t = u_ref.shape[0]
    u = cg_ref[...] * u_ref[...]
    st = st_ref[...]
    row = lax.broadcasted_iota(jnp.int32, u.shape, 0)
    u1 = jnp.where(row == 0, st[1:2, :], pltpu.roll(u, 1, 0))
    u2 = jnp.where(row == 0, st[0:1, :], jnp.where(row == 1, st[1:2, :], pltpu.roll(u, 2, 0)))
    w = w_ref[...]
    y = u2 * w[0:1, :] + u1 * w[1:2, :] + u * w[2:3, :]
    y_ref[...] = (bg_ref[...] * y).astype(y_ref.dtype)
    ns_ref[...] = u[t - (CONV_W - 1):t, :]


def _short_conv(z, state, w, *, n_batch, t, row0):
    tc = 256
    nb = row0 // t
    col = lambda c0: pl.BlockSpec((t, tc), lambda bi, j: (nb + bi, c0 // tc + j))
    return pl.pallas_call(
        _conv_kernel,
        grid=(n_batch, B_WIDTH // tc),
        in_specs=[col(AB_BG), col(AB_CG), col(AB_BIN),
                  pl.BlockSpec((None, CONV_W - 1, tc), lambda bi, j: (bi, 0, j)),
                  pl.BlockSpec((CONV_W, tc), lambda bi, j: (0, j))],
        out_specs=[pl.BlockSpec((t, tc), lambda bi, j: (bi, j)),
                   pl.BlockSpec((None, CONV_W - 1, tc), lambda bi, j: (bi, 0, j))],
        out_shape=[jax.ShapeDtypeStruct((n_batch * t, B_WIDTH), BF16),
                   jax.ShapeDtypeStruct((n_batch, CONV_W - 1, B_WIDTH), F32)],
        compiler_params=_cparams(("parallel", "parallel")),
        name="short_conv",
    )(z, z, z, state, w)


def _cumsum_rows(x):
    n = x.shape[0]
    row = lax.broadcasted_iota(jnp.int32, x.shape, 0)
    s = 1
    while s < n:
        x = x + jnp.where(row >= s, pltpu.roll(x, s, 0), 0.0)
        s *= 2
    return x


def _bcast_rows(x, period, offset):
    n, w = x.shape
    return jnp.concatenate(
        [jnp.broadcast_to(x[b * period + offset:b * period + offset + 1, :], (period, w)) for b in range(n // period)],
        axis=0)


def _hgrn_kernel(q_ref, f_ref, v_ref, g_ref, lb_ref, gain_ref, s0_ref, o_ref, s_out_ref, st_scr, *, tc):
    step = pl.program_id(2)
    tb = q_ref.shape[0]
    nt = (((1,), (1,)), ((), ()))
    tn = (((0,), (0,)), ((), ()))

    @pl.when(step == 0)
    def _():
        st_scr[...] = s0_ref[...].T

    lbv = lb_ref[...]
    row = lax.broadcasted_iota(jnp.int32, (tc, C_DK), 0)
    trow = lax.broadcasted_iota(jnp.int32, (tc, tc), 0)
    tcol = lax.broadcasted_iota(jnp.int32, (tc, tc), 1)
    levels = [m for m in (32, 16, 8) if 2 * m <= tc]

    def chunk(c, carry):
        rows = pl.ds(pl.multiple_of(c * tc, tc), tc)
        f = lbv + (1.0 - lbv) * jax.nn.sigmoid(f_ref[rows, :])
        kk = 1.0 - f
        a = _cumsum_rows(jnp.log(f))
        qr = q_ref[rows, :]
        q = qr * jax.nn.sigmoid(qr)
        v = v_ref[rows, :].astype(BF16)
        st = st_scr[...]

        o = lax.dot_general((q * jnp.exp(a)).astype(BF16), st.astype(BF16), nt, preferred_element_type=F32)

        sc = jnp.zeros((tc, tc), F32)
        for m in levels:
            ref = _bcast_rows(a, 2 * m, m - 1)
            upper = jnp.bitwise_and(row, 2 * m - 1) >= m
            e = jnp.exp(jnp.where(upper, a - ref, ref - a))
            qm = jnp.where(upper, q * e, 0.0).astype(BF16)
            km = jnp.where(upper, 0.0, kk * e).astype(BF16)
            sm = lax.dot_general(qm, km, nt, preferred_element_type=F32)
            shift = (2 * m).bit_length() - 1
            same = jnp.right_shift(trow, shift) == jnp.right_shift(tcol, shift)
            sc = sc + jnp.where(same, sm, 0.0)
        for s in range(SUBLANES):
            ok = jnp.bitwise_and(row, SUBLANES - 1) >= s
            d = jnp.where(ok, a - _bcast_rows(a, SUBLANES, s), 0.0)
            w = q * _bcast_rows(kk, SUBLANES, s) * jnp.exp(d)
            col = jnp.sum(w, axis=-1, keepdims=True)
            here = jnp.logical_and(tcol == jnp.bitwise_and(trow, -SUBLANES) + s,
                                   jnp.bitwise_and(trow, SUBLANES - 1) >= s)
            sc = sc + jnp.where(here, col, 0.0)

        o = o + jnp.dot(sc.astype(BF16), v, preferred_element_type=F32)

        a_last = a[tc - 1:tc, :]
        kd = (kk * jnp.exp(a_last - a)).astype(BF16)
        st_scr[...] = st * jnp.exp(a_last) + lax.dot_general(v, kd, tn, preferred_element_type=F32)

        ms = jnp.mean(o * o, axis=-1, keepdims=True)
        gr = g_ref[rows, :]
        y = o * lax.rsqrt(ms + NORM_EPS) * gain_ref[...] * (gr * jax.nn.sigmoid(gr))
        o_ref[rows, :] = y.astype(o_ref.dtype)
        return carry

    lax.fori_loop(0, tb // tc, chunk, 0, unroll=min(HGRN_UNROLL, tb // tc))

    @pl.when(step == pl.num_programs(2) - 1)
    def _():
        s_out_ref[...] = st_scr[...].T


def _hgrn2(z, lb, gain, s0, *, n_batch, t, row0):
    tc = min(CHUNK, t)
    tb = min(512, t)
    nt = t // tb
    hb = CD_W // C_DK
    xspec = lambda seg: pl.BlockSpec((tb, C_DK), lambda bi, h, i: (row0 // tb + bi * nt + i, seg * hb + h))
    smap = lambda bi, h, i: (bi, h, 0, 0)
    return pl.pallas_call(
        functools.partial(_hgrn_kernel, tc=tc),
        grid=(n_batch, C_HEADS, nt),
        in_specs=[xspec(0), xspec(1), xspec(2), xspec(3),
                  pl.BlockSpec((1, C_DK), lambda bi, h, i: (0, h)),
                  pl.BlockSpec((1, C_DV), lambda bi, h, i: (0, 0)),
                  pl.BlockSpec((None, None, C_DK, C_DV), smap)],
        out_specs=[pl.BlockSpec((tb, C_DV), lambda bi, h, i: (bi * nt + i, h)),
                   pl.BlockSpec((None, None, C_DK, C_DV), smap)],
        out_shape=[jax.ShapeDtypeStruct((n_batch * t, CD_W), BF16), jax.ShapeDtypeStruct(s0.shape, F32)],
        scratch_shapes=[pltpu.VMEM((C_DV, C_DK), F32)],
        compiler_params=_cparams(("parallel", "parallel", "arbitrary")),
        name="hgrn2",
    )(z, z, z, z, lb, gain, s0)


def _cumsum_kernel(x_ref, o_ref):
    o_ref[...] = _cumsum_rows(x_ref[...])


def _cumsum_time(x):
    b, l, w = x.shape
    return pl.pallas_call(
        _cumsum_kernel,
        grid=(b,),
        in_specs=[pl.BlockSpec((None, l, w), lambda i: (i, 0, 0))],
        out_specs=pl.BlockSpec((None, l, w), lambda i: (i, 0, 0)),
        out_shape=jax.ShapeDtypeStruct(x.shape, F32),
        compiler_params=_cparams(("parallel",)),
        name="cumsum_time",
    )(x)


def _prep_d_kernel(zq_ref, zk_ref, zf_ref, gq_ref, gk_ref, fb_ref, q_ref, k_ref, kb_ref, lf_ref):
    for h in range(D_HEADS):
        cols = slice(h * HEAD_DIM, (h + 1) * HEAD_DIM)
        q_ref[:, cols] = (_head_rms(zq_ref[:, cols], gq_ref[...]) * (HEAD_DIM ** -0.5)).astype(BF16)
        y = _head_rms(zk_ref[:, cols], gk_ref[...])
        k_ref[:, cols] = y
        kb_ref[:, cols] = y.astype(BF16)
    x = zf_ref[:, 0:LANES] + fb_ref[...]
    lf_ref[...] = jnp.minimum(x, 0.0) - jnp.log(1.0 + jnp.exp(-jnp.abs(x)))


def _prep_d(za, zb, gq, gk, fb):
    m = za.shape[0]
    tm = _pick_tile(m, (256, 128, 96, 64, 32, 16))
    row = lambda w, c: pl.BlockSpec((tm, w), lambda i: (i, c))
    full = lambda w: pl.BlockSpec((tm, w), lambda i: (i, 0))
    par = pl.BlockSpec((1, LANES), lambda i: (0, 0))
    sds = lambda w, dt: jax.ShapeDtypeStruct((m, w), dt)
    return pl.pallas_call(
        _prep_d_kernel,
        grid=(m // tm,),
        in_specs=[row(CD_W, 4), row(CD_W, 5), row(N_ALIGN, CD_FD // N_ALIGN), par, par, par],
        out_specs=[full(CD_W), full(CD_W), full(CD_W), full(LANES)],
        out_shape=[sds(CD_W, BF16), sds(CD_W, F32), sds(CD_W, BF16), sds(LANES, F32)],
        compiler_params=_cparams(("parallel",)),
        name="prep_d",
    )(za, za, zb, gq, gk, fb)


def _fox_kernel(q_ref, k_ref, v_ref, cq_ref, ck_ref, g_ref, o_ref, *, tq):
    t = q_ref.shape[0]
    dn = (((1,), (1,)), ((), ()))
    row = lax.broadcasted_iota(jnp.int32, (tq, tq), 0)
    col = lax.broadcasted_iota(jnp.int32, (tq, tq), 1)
    for qi in range(t // tq):
        lo, hi = qi * tq, (qi + 1) * tq
        q = q_ref[lo:hi, :]
        cq = cq_ref[lo:hi, :]
        s_d = lax.dot_general(q, k_ref[lo:hi, :], dn, preferred_element_type=F32) + cq - ck_ref[:, lo:hi]
        s_d = jnp.where(col <= row, s_d, -jnp.inf)
        m = jnp.max(s_d, axis=-1, keepdims=True)
        if qi:
            s_o = lax.dot_general(q, k_ref[0:lo, :], dn, preferred_element_type=F32) + cq - ck_ref[:, 0:lo]
            m = jnp.maximum(m, jnp.max(s_o, axis=-1, keepdims=True))
        p_d = jnp.exp(s_d - m)
        l = jnp.sum(p_d, axis=-1, keepdims=True)
        acc = jnp.dot(p_d.astype(BF16), v_ref[lo:hi, :].astype(BF16), preferred_element_type=F32)
        if qi:
            p_o = jnp.exp(s_o - m)
            l = l + jnp.sum(p_o, axis=-1, keepdims=True)
            acc = acc + jnp.dot(p_o.astype(BF16), v_ref[0:lo, :].astype(BF16), preferred_element_type=F32)
        o = acc / l * jax.nn.sigmoid(g_ref[lo:hi, :])
        o_ref[lo:hi, :] = o.astype(o_ref.dtype)


def _fox_prefill(q, k, za, zb, cum_col, cum_row, *, n_batch, t):
    hb = CD_W // HEAD_DIM
    tq = min(FOX_TQ, t)
    qmap = lambda bi, h: (bi, h)
    return pl.pallas_call(
        functools.partial(_fox_kernel, tq=tq),
        grid=(n_batch, D_HEADS),
        in_specs=[pl.BlockSpec((t, HEAD_DIM), qmap),
                  pl.BlockSpec((t, HEAD_DIM), qmap),
                  pl.BlockSpec((t, HEAD_DIM), lambda bi, h: (bi, 6 * hb + h)),
                  pl.BlockSpec((None, None, t, 1), lambda bi, h: (bi, h, 0, 0)),
                  pl.BlockSpec((None, None, 1, t), lambda bi, h: (bi, h, 0, 0)),
                  pl.BlockSpec((t, HEAD_DIM), lambda bi, h: (bi, h))],
        out_specs=pl.BlockSpec((t, HEAD_DIM), qmap),
        out_shape=jax.ShapeDtypeStruct((n_batch * t, CD_W), BF16),
        compiler_params=_cparams(("parallel", "parallel")),
        name="fox_prefill",
    )(q, k, za, cum_col, cum_row, zb)


def _fox_dec_kernel(q_ref, kc_ref, vc_ref, kn_ref, vn_ref, cq_ref, ckc_ref, ckn_ref, g_ref, o_ref):
    t = q_ref.shape[0]
    q = q_ref[...]
    dn = (((1,), (1,)), ((), ()))
    s_c = lax.dot_general(q, kc_ref[...].astype(BF16), dn, preferred_element_type=F32)
    s_c = s_c + cq_ref[...] - ckc_ref[...]
    s_n = lax.dot_general(q, kn_ref[...], dn, preferred_element_type=F32)
    s_n = s_n + cq_ref[...] - ckn_ref[...]
    row = lax.broadcasted_iota(jnp.int32, (t, t), 0)
    col = lax.broadcasted_iota(jnp.int32, (t, t), 1)
    s_n = jnp.where(col <= row, s_n, -jnp.inf)
    m = jnp.maximum(jnp.max(s_c, axis=-1, keepdims=True), jnp.max(s_n, axis=-1, keepdims=True))
    p_c = jnp.exp(s_c - m)
    p_n = jnp.exp(s_n - m)
    l = jnp.sum(p_c, axis=-1, keepdims=True) + jnp.sum(p_n, axis=-1, keepdims=True)
    acc = jnp.dot(p_c.astype(BF16), vc_ref[...].astype(BF16), preferred_element_type=F32)
    acc = acc + jnp.dot(p_n.astype(BF16), vn_ref[...].astype(BF16), preferred_element_type=F32)
    o_ref[...] = (acc / l * jax.nn.sigmoid(g_ref[...])).astype(o_ref.dtype)


def _fox_decode(q, k, za, zb, k_cache, v_cache, cq_col, ck_cache_row, ck_new_row, *, n_batch, t, row0):
    p = k_cache.shape[1]
    hb = CD_W // HEAD_DIM
    r0 = row0 // t
    new = lambda seg: pl.BlockSpec((t, HEAD_DIM), lambda bi, h: (r0 + bi, seg * hb + h))
    cache = pl.BlockSpec((None, p, HEAD_DIM), lambda bi, h: (bi, 0, h))
    return pl.pallas_call(
        _fox_dec_kernel,
        grid=(n_batch, D_HEADS),
        in_specs=[new(0), cache, cache, new(0), new(6),
                  pl.BlockSpec((None, None, t, 1), lambda bi, h: (bi, h, 0, 0)),
                  pl.BlockSpec((None, None, 1, p), lambda bi, h: (bi, h, 0, 0)),
                  pl.BlockSpec((None, None, 1, t), lambda bi, h: (bi, h, 0, 0)),
                  new(0)],
        out_specs=pl.BlockSpec((t, HEAD_DIM), lambda bi, h: (bi, h)),
        out_shape=jax.ShapeDtypeStruct((n_batch * t, CD_W), BF16),
        compiler_params=_cparams(("parallel", "parallel")),
        name="fox_decode",
    )(q, k_cache, v_cache, k, za, cq_col, ck_cache_row, ck_new_row, zb)


def _pad_cols(w, n):
    return jnp.pad(w, ((0, 0), (0, n - w.shape[1])))


def _ab_tail_weight(w):
    o = np.cumsum((0,) + AB_SIZES)
    return jnp.concatenate([w[:, o[6]:].astype(BF16), _pad_cols(w[:, o[4]:o[6]].astype(BF16), N_ALIGN)], axis=1)


def _cd_tail_weight(w):
    o = np.cumsum((0,) + CD_SIZES)
    return jnp.concatenate([w[:, o[8]:].astype(BF16), _pad_cols(w[:, o[7]:o[8]].astype(BF16), N_ALIGN)], axis=1)


def _expand_groups(mod_p, mod_s, seq_p):
    return jnp.concatenate([jnp.repeat(mod_p, seq_p // ROW_GROUP, axis=0), mod_s], axis=0)


def _lane_row(v):
    return jnp.pad(v.astype(F32), (0, LANES - v.shape[0]))[None, :]


def kernel(x_prompt, x_sample, c_prompt, c_sample, cache_a_k, cache_a_v, cache_a_idx_k, state_b_conv, state_c_s, cache_d_k, cache_d_v, cache_d_logf, ada_w, ada_b, norm_mix, norm_ffn, ab_w_in, ab_w_out, a_q_norm, a_k_norm, b_conv_w, cd_w_in, cd_w_out, c_lb, c_o_norm, d_q_norm, d_k_norm, d_f_bias, ffn_w_gate, ffn_w_up, ffn_w_down):
    bp, tp, d = x_prompt.shape
    bs, ts, _ = x_sample.shape
    past = cache_a_k.shape[2]
    depth = ada_w.shape[0]
    assert ts == ROW_GROUP and tp % ROW_GROUP == 0 and d == D_MODEL
    mp, ms = bp * tp, bs * ts

    x = jnp.concatenate([x_prompt.reshape(mp, d), x_sample.reshape(ms, d)], axis=0)
    c_all = jnp.concatenate([c_prompt, c_sample], axis=0)
    c_rows = -(-c_all.shape[0] // SUBLANES) * SUBLANES
    c_pad = jnp.pad(c_all, ((0, c_rows - c_all.shape[0]), (0, 0)))

    lb_all = jnp.cumsum(jax.nn.softmax(c_lb.astype(F32), axis=0), axis=0)
    lb_all = lb_all - lb_all[0]

    pos = jnp.concatenate([jnp.tile(jnp.arange(tp), bp), past + jnp.tile(jnp.arange(ts), bs)])
    cos_q, sin_q = _rope_tables(pos, HEAD_DIM)
    cos_i, sin_i = _rope_tables(pos, IDX_DIM)

    outs_p, outs_s = {}, {}
    mod_all = _ada_mod(c_pad, ada_w, ada_b[:, None, :])
    w_down = ffn_w_down.astype(BF16)
    SH1, SC1, G1, SH2, SC2, G2 = range(6)
    for l in range(depth):
        g16 = _expand_groups(mod_all[l, :bp], mod_all[l, bp:bp + bs], tp)
        h = _norm_mod(x, norm_mix[l][None, :], g16, SC1, SH1)
        if l % 2 == 0:
            e = l // 2
            za = _matmul_head(h, ab_w_in, e, AB_HEAD)
            z = _matmul(h, _ab_tail_weight(ab_w_in[e]))
            q_b, qi_b, k_f, k_b, v_b, ki_f, ki_lo, ki_hi = _prep_a(
                za, z, cos_q, sin_q, cos_i, sin_i, a_q_norm[e][None, :], a_k_norm[e][None, :])
            v_f = za[:, AB_V:AB_V + A_KV_HEADS * HEAD_DIM]

            keys_p = [a[:mp].reshape(bp, tp, -1) for a in (ki_lo, ki_hi, k_b, v_b)]
            grp = DSA_GROUP if tp % DSA_GROUP == 0 else tp
            ya_groups = []
            for g0 in range(0, tp, grp):
                o = _dsa_attention(qi_b, keys_p[0], keys_p[1], z, q_b, keys_p[2], keys_p[3], n_batch=bp, t=grp,
                                   tq=min(Q_BLOCK, grp), row0=g0, q_stride=tp, lp=g0 + grp, q_pos0=g0,
                                   l_valid=g0 + grp, n_sel=min(TOPK_MAX, tp // 4))
                ya_groups.append(o.reshape(bp, grp, -1))
            ya_p = jnp.concatenate(ya_groups, axis=1).reshape(mp, -1)
            ls = past + ts
            lp = -(-ls // LANES) * LANES

            def with_cache(cache, new_rows):
                cache = cache.reshape(bs, past, -1).astype(BF16)
                new_rows = new_rows[mp:].reshape(bs, ts, -1)
                full = jnp.concatenate([cache, new_rows], axis=1)
                return jnp.pad(full, ((0, 0), (0, lp - ls), (0, 0)))

            zi = jnp.zeros_like(cache_a_idx_k[e])
            ya_s = _dsa_attention(
                qi_b, with_cache(jnp.concatenate([cache_a_idx_k[e], zi], -1), ki_lo),
                with_cache(jnp.concatenate([zi, cache_a_idx_k[e]], -1), ki_hi), z, q_b,
                with_cache(cache_a_k[e], k_b), with_cache(cache_a_v[e], v_b),
                n_batch=bs, t=ts, tq=ts, row0=mp, q_stride=ts, lp=lp, q_pos0=past, l_valid=ls,
                n_sel=min(TOPK_MAX, ls // 4))

            yb_p, conv_p = _short_conv(z, jnp.zeros((bp, CONV_W - 1, B_WIDTH), F32), b_conv_w[e],
                                       n_batch=bp, t=tp, row0=0)
            yb_s, conv_s = _short_conv(z, state_b_conv[e], b_conv_w[e], n_batch=bs, t=ts, row0=mp)
            x = _mix_out(jnp.concatenate([ya_p, ya_s], axis=0), jnp.concatenate([yb_p, yb_s], axis=0),
                         ab_w_out[e], x, g16, G1)

            for od, sl, bn, t, conv_new in ((outs_p, slice(0, mp), bp, tp, conv_p),
                                            (outs_s, slice(mp, mp + ms), bs, ts, conv_s)):
                od.setdefault("a_k", []).append(k_f[sl].reshape(bn, t, A_KV_HEADS, HEAD_DIM))
                od.setdefault("a_v", []).append(v_f[sl].reshape(bn, t, A_KV_HEADS, HEAD_DIM))
                od.setdefault("a_ik", []).append(ki_f[sl].reshape(bn, t, IDX_DIM))
                od.setdefault("b_conv", []).append(conv_new)
        else:
            od_ = l // 2
            za = _matmul_head(h, cd_w_in, od_, CD_HEAD)
            z = _matmul(h, _cd_tail_weight(cd_w_in[od_]))
            q_b, k_f, k_b, lf_slab = _prep_d(za, z, d_q_norm[od_][None, :], d_k_norm[od_][None, :],
                                             _lane_row(d_f_bias[od_]))
            logf = lf_slab[:, :D_HEADS]
            v_f = za[:, 6 * CD_W:7 * CD_W]
            lb = lb_all[l][None, :]
            gain = c_o_norm[od_][None, :]

            yc_p, s_p = _hgrn2(za, lb, gain, jnp.zeros((bp, C_HEADS, C_DK, C_DV), F32), n_batch=bp, t=tp, row0=0)
            yc_s, s_s = _hgrn2(za, lb, gain, state_c_s[od_], n_batch=bs, t=ts, row0=mp)

            cum_p = _cumsum_time(lf_slab[:mp].reshape(bp, tp, LANES))[:, :, :D_HEADS].transpose(0, 2, 1)
            yd_p = _fox_prefill(q_b, k_b, za, z, cum_p[..., None], cum_p[:, :, None, :], n_batch=bp, t=tp)
            lf_cache = jnp.pad(cache_d_logf[od_], ((0, 0), (0, 0), (0, LANES - D_HEADS)))
            cum_s = _cumsum_time(jnp.concatenate([lf_cache, lf_slab[mp:].reshape(bs, ts, LANES)], axis=1))
            cum_s = cum_s[:, :, :D_HEADS].transpose(0, 2, 1)
            yd_s = _fox_decode(q_b, k_b, za, z, cache_d_k[od_].astype(BF16).reshape(bs, past, CD_W),
                               cache_d_v[od_].astype(BF16).reshape(bs, past, CD_W), cum_s[:, :, past:, None],
                               cum_s[:, :, None, :past], cum_s[:, :, None, past:], n_batch=bs, t=ts, row0=mp)
            x = _mix_out(jnp.concatenate([yc_p, yc_s], axis=0), jnp.concatenate([yd_p, yd_s], axis=0),
                         cd_w_out[od_], x, g16, G1)

            for od, sl, bn, t, s_new in ((outs_p, slice(0, mp), bp, tp, s_p),
                                         (outs_s, slice(mp, mp + ms), bs, ts, s_s)):
                od.setdefault("c_s", []).append(s_new)
                od.setdefault("d_k", []).append(k_f[sl].reshape(bn, t, D_HEADS, HEAD_DIM))
                od.setdefault("d_v", []).append(v_f[sl].reshape(bn, t, D_HEADS, HEAD_DIM))
                od.setdefault("d_lf", []).append(logf[sl].reshape(bn, t, D_HEADS))
        h2 = _norm_mod(x, norm_ffn[l][None, :], g16, SC2, SH2)
        hid = _ffn_gate_up(h2, ffn_w_gate, ffn_w_up, l)
        x = _ffn_down(hid, w_down, l, x, g16, G2)

    y_prompt = x[:mp].reshape(bp, tp, d)
    y_sample = x[mp:].reshape(bs, ts, d)
    names = ("a_k", "a_v", "a_ik", "b_conv", "c_s", "d_k", "d_v", "d_lf")
    return ((y_prompt, y_sample) + tuple(jnp.stack(outs_p[n]) for n in names)
            + tuple(jnp.stack(outs_s[n]) for n in names))
```

```python
import functools

import numpy as np
import jax
import jax.numpy as jnp
from jax import lax
from jax.experimental import pallas as pl
from jax.experimental.pallas import tpu as pltpu

F32 = jnp.float32
BF16 = jnp.bfloat16

D_MODEL = 4096
CHUNK = 64
Q_BLOCK = 128
HEAD_DIM = 128
ROPE_THETA = 500000.0
ROT_FRACTION = 4
NORM_EPS = 1e-6
A_HEADS = D_MODEL // 256
A_KV_HEADS = A_HEADS // 4
IDX_HEADS = D_MODEL // 128
IDX_DIM = 64
TOPK_MAX = 256
B_WIDTH = D_MODEL // 2
CONV_W = 3
C_HEADS = D_MODEL // 256
C_DK = 128
C_DV = 128
D_HEADS = D_MODEL // 256
AB_SIZES = (A_HEADS * HEAD_DIM, A_KV_HEADS * HEAD_DIM, A_KV_HEADS * HEAD_DIM,
            IDX_HEADS * IDX_DIM, IDX_DIM, IDX_HEADS, B_WIDTH, B_WIDTH, B_WIDTH)
CD_SIZES = (C_HEADS * C_DK, C_HEADS * C_DK, C_HEADS * C_DV, C_HEADS * C_DV,
            D_HEADS * HEAD_DIM, D_HEADS * HEAD_DIM, D_HEADS * HEAD_DIM, D_HEADS, D_HEADS * HEAD_DIM)

LANES = 128
SUBLANES = 8
VMEM_LIMIT_BYTES = 56 * 1024 * 1024

ROW_GROUP = 16
N_ALIGN = 512
INT_MIN = -2 ** 31
HGRN_UNROLL = 4
FOX_TQ = 256
FOX_DEC_POS = 512
DSA_GROUP = 512

AB_Q, AB_K, AB_V, AB_QI, AB_HEAD = 0, 2048, 2560, 3072, 5120
AB_BG, AB_CG, AB_BIN, AB_KW = 0, 2048, 4096, 6144
CD_W = 2048
CD_HEAD = 7 * CD_W
CD_FD = CD_W


def _cparams(sem):
    return pltpu.CompilerParams(dimension_semantics=sem, vmem_limit_bytes=VMEM_LIMIT_BYTES)


def _pick_tile(n, candidates):
    for c in candidates:
        if n % c == 0:
            return c
    raise ValueError(f"no tile for {n}")


M_TILES = (768, 512, 256, 128, 96, 64, 32, 16)


def _ada_kernel(c_ref, w_ref, b_ref, o_ref):
    c = c_ref[...]
    a = (c * jax.nn.sigmoid(c)).astype(BF16)
    o_ref[...] = jnp.dot(a, w_ref[...].astype(BF16), preferred_element_type=F32) + b_ref[...]


def _ada_mod(c, w, b):
    r, d = c.shape
    depth, _, n = w.shape
    tn = _pick_tile(n, (512, 256, 128))
    return pl.pallas_call(
        _ada_kernel,
        grid=(depth, n // tn),
        in_specs=[pl.BlockSpec((r, d), lambda l, j: (0, 0)),
                  pl.BlockSpec((None, d, tn), lambda l, j: (l, 0, j)),
                  pl.BlockSpec((None, 1, tn), lambda l, j: (l, 0, j))],
        out_specs=pl.BlockSpec((None, r, tn), lambda l, j: (l, 0, j)),
        out_shape=jax.ShapeDtypeStruct((depth, r, n), F32),
        compiler_params=_cparams(("parallel", "parallel")),
        name="ada_mod",
    )(c, w, b)


def _norm_mod_kernel(x_ref, gain_ref, sc_ref, sh_ref, o_ref, *, groups):
    gain = gain_ref[...]
    for g in range(groups):
        rows = slice(g * ROW_GROUP, (g + 1) * ROW_GROUP)
        x = x_ref[rows, :]
        ms = jnp.mean(x * x, axis=-1, keepdims=True)
        y = x * lax.rsqrt(ms + NORM_EPS) * gain
        o_ref[rows, :] = (y * (1.0 + sc_ref[g:g + 1, :]) + sh_ref[g:g + 1, :]).astype(o_ref.dtype)


def _norm_mod(x, gain, g16, sc_seg, sh_seg):
    m, d = x.shape
    tm = _pick_tile(m, (384,) + M_TILES)
    groups = tm // ROW_GROUP
    return pl.pallas_call(
        functools.partial(_norm_mod_kernel, groups=groups),
        grid=(m // tm,),
        in_specs=[pl.BlockSpec((tm, d), lambda i: (i, 0)),
                  pl.BlockSpec((1, d), lambda i: (0, 0)),
                  pl.BlockSpec((groups, d), lambda i: (i, sc_seg)),
                  pl.BlockSpec((groups, d), lambda i: (i, sh_seg))],
        out_specs=pl.BlockSpec((tm, d), lambda i: (i, 0)),
        out_shape=jax.ShapeDtypeStruct((m, d), BF16),
        compiler_params=_cparams(("parallel",)),
        name="norm_mod",
    )(x, gain, g16, g16)


def _mm_kernel(a_ref, w_ref, o_ref):
    o_ref[...] = jnp.dot(a_ref[...], w_ref[...], preferred_element_type=F32).astype(o_ref.dtype)


def _matmul(a, w):
    m, k = a.shape
    n = w.shape[1]
    tm = _pick_tile(m, M_TILES)
    tn = _pick_tile(n, (512, 256, 128))
    return pl.pallas_call(
        _mm_kernel,
        grid=(m // tm, n // tn),
        in_specs=[pl.BlockSpec((tm, k), lambda i, j: (i, 0)),
                  pl.BlockSpec((k, tn), lambda i, j: (0, j))],
        out_specs=pl.BlockSpec((tm, tn), lambda i, j: (i, j)),
        out_shape=jax.ShapeDtypeStruct((m, n), F32),
        compiler_params=_cparams(("parallel", "parallel")),
        name="proj_in",
    )(a, w)


def _mm_w32_kernel(a_ref, w_ref, o_ref):
    o_ref[...] = jnp.dot(a_ref[...], w_ref[...].astype(BF16), preferred_element_type=F32)


def _matmul_head(a, w, layer, n_cols):
    m, k = a.shape
    tm = _pick_tile(m, (1056,) + M_TILES)
    tn = _pick_tile(n_cols, (512, 256, 128))
    return pl.pallas_call(
        _mm_w32_kernel,
        grid=(m // tm, n_cols // tn),
        in_specs=[pl.BlockSpec((tm, k), lambda i, j: (i, 0)),
                  pl.BlockSpec((None, k, tn), lambda i, j: (layer, 0, j))],
        out_specs=pl.BlockSpec((tm, tn), lambda i, j: (i, j)),
        out_shape=jax.ShapeDtypeStruct((m, n_cols), F32),
        compiler_params=_cparams(("parallel", "parallel")),
        name="proj_in_head",
    )(a, w)


def _gated_residual(o_ref, x_ref, g_ref, acc, groups):
    for g in range(groups):
        rows = slice(g * ROW_GROUP, (g + 1) * ROW_GROUP)
        o_ref[rows, :] = x_ref[rows, :] + g_ref[g:g + 1, :] * acc[rows, :]


def _mix_out_kernel(a1_ref, a2_ref, w1_ref, w2_ref, x_ref, g_ref, o_ref, *, groups):
    acc = jnp.dot(a1_ref[...], w1_ref[...], preferred_element_type=F32)
    acc = acc + jnp.dot(a2_ref[...], w2_ref[...], preferred_element_type=F32)
    _gated_residual(o_ref, x_ref, g_ref, acc, groups)


def _mix_out(a1, a2, w, x, g16, gate_seg):
    m, kh = a1.shape
    n = w.shape[1]
    assert w.shape[0] == 2 * kh
    tm = _pick_tile(m, M_TILES)
    tn = _pick_tile(n, (512, 256, 128))
    groups = tm // ROW_GROUP
    return pl.pallas_call(
        functools.partial(_mix_out_kernel, groups=groups),
        grid=(m // tm, n // tn),
        in_specs=[pl.BlockSpec((tm, kh), lambda i, j: (i, 0)),
                  pl.BlockSpec((tm, kh), lambda i, j: (i, 0)),
                  pl.BlockSpec((kh, tn), lambda i, j: (0, j)),
                  pl.BlockSpec((kh, tn), lambda i, j: (1, j)),
                  pl.BlockSpec((tm, tn), lambda i, j: (i, j)),
                  pl.BlockSpec((groups, tn), lambda i, j: (i, gate_seg * (n // tn) + j))],
        out_specs=pl.BlockSpec((tm, tn), lambda i, j: (i, j)),
        out_shape=jax.ShapeDtypeStruct((m, n), F32),
        compiler_params=_cparams(("parallel", "parallel")),
        name="mix_out_resid",
    )(a1, a2, w, w, x, g16)


def _ffn_down_kernel(a_ref, w_ref, x_ref, g_ref, o_ref, acc_ref, *, nk, groups):
    k = pl.program_id(1)
    j = pl.program_id(2)
    part = jnp.dot(a_ref[...], w_ref[...], preferred_element_type=F32)
    if nk == 1:
        _gated_residual(o_ref, x_ref, g_ref, part, groups)
    else:
        @pl.when(k == 0)
        def _():
            acc_ref[j] = part

        @pl.when(jnp.logical_and(k > 0, k < nk - 1))
        def _():
            acc_ref[j] += part

        @pl.when(k == nk - 1)
        def _():
            _gated_residual(o_ref, x_ref, g_ref, acc_ref[j] + part, groups)


def _ffn_down(a, w, layer, x, g16, gate_seg):
    m, k = a.shape
    n = w.shape[2]
    tm = _pick_tile(m, M_TILES)
    tn = _pick_tile(n, (512, 256, 128))
    tk = k if k <= 4096 else _pick_tile(k, (5504, 4096, 2048, 1024))
    nk = k // tk
    groups = tm // ROW_GROUP
    last = lambda kk, j: jnp.where(kk == nk - 1, j, 0)
    return pl.pallas_call(
        functools.partial(_ffn_down_kernel, nk=nk, groups=groups),
        grid=(m // tm, nk, n // tn),
        in_specs=[pl.BlockSpec((tm, tk), lambda i, kk, j: (i, kk)),
                  pl.BlockSpec((None, tk, tn), lambda i, kk, j: (layer, kk, j)),
                  pl.BlockSpec((tm, tn), lambda i, kk, j: (i, last(kk, j))),
                  pl.BlockSpec((groups, tn), lambda i, kk, j: (i, gate_seg * (n // tn) + last(kk, j)))],
        out_specs=pl.BlockSpec((tm, tn), lambda i, kk, j: (i, last(kk, j))),
        out_shape=jax.ShapeDtypeStruct((m, n), F32),
        scratch_shapes=[pltpu.VMEM((n // tn, tm, tn), F32)],
        compiler_params=_cparams(("parallel", "arbitrary", "arbitrary")),
        name="ffn_down_resid",
    )(a, w, x, g16)


def _ffn_gate_up_kernel(a_ref, wg_ref, wu_ref, o_ref):
    a = a_ref[...]
    g = jnp.dot(a, wg_ref[...].astype(BF16), preferred_element_type=F32)
    u = jnp.dot(a, wu_ref[...].astype(BF16), preferred_element_type=F32)
    o_ref[...] = (g * jax.nn.sigmoid(g) * u).astype(o_ref.dtype)


def _ffn_gate_up(a, wg, wu, layer):
    m, k = a.shape
    n = wg.shape[2]
    tm = _pick_tile(m, (1056,) + M_TILES)
    tn = _pick_tile(n, (256, 128))
    return pl.pallas_call(
        _ffn_gate_up_kernel,
        grid=(m // tm, n // tn),
        in_specs=[pl.BlockSpec((tm, k), lambda i, j: (i, 0)),
                  pl.BlockSpec((None, k, tn), lambda i, j: (layer, 0, j)),
                  pl.BlockSpec((None, k, tn), lambda i, j: (layer, 0, j))],
        out_specs=pl.BlockSpec((tm, tn), lambda i, j: (i, j)),
        out_shape=jax.ShapeDtypeStruct((m, n), BF16),
        compiler_params=_cparams(("parallel", "parallel")),
        name="ffn_gate_up",
    )(a, wg, wu)


def _rope_lanes(y, cos, sin_signed, half, period):
    lane = lax.broadcasted_iota(jnp.int32, y.shape, 1)
    first = jnp.bitwise_and(lane, period - 1) < half
    swap = jnp.where(first, pltpu.roll(y, LANES - half, 1), pltpu.roll(y, half, 1))
    return y * cos + swap * sin_signed


def _head_rms(x, gain):
    ms = jnp.mean(x * x, axis=-1, keepdims=True)
    return x * lax.rsqrt(ms + NORM_EPS) * gain


def _prep_a_kernel(zq_ref, zqi0_ref, zqi1_ref, zkv_ref, zkw_ref, cq_ref, sq_ref, ci_ref, si_ref, gq_ref, gk_ref,
                   q_ref, qi_ref, k_ref, kb_ref, vb_ref, ki_ref, kilo_ref, kihi_ref):
    cq, sq, ci, si = cq_ref[...], sq_ref[...], ci_ref[...], si_ref[...]
    half_q = HEAD_DIM // ROT_FRACTION // 2
    half_i = IDX_DIM // ROT_FRACTION // 2
    for h in range(A_HEADS):
        cols = slice(h * HEAD_DIM, (h + 1) * HEAD_DIM)
        y = _rope_lanes(_head_rms(zq_ref[:, cols], gq_ref[...]), cq, sq, half_q, HEAD_DIM)
        q_ref[:, cols] = (y * (HEAD_DIM ** -0.5)).astype(BF16)
    for h in range(A_KV_HEADS):
        cols = slice(h * HEAD_DIM, (h + 1) * HEAD_DIM)
        y = _rope_lanes(_head_rms(zkv_ref[:, cols], gk_ref[...]), cq, sq, half_q, HEAD_DIM)
        k_ref[:, cols] = y
        kb_ref[:, cols] = y.astype(BF16)
    nkv = A_KV_HEADS * HEAD_DIM
    vb_ref[...] = zkv_ref[:, nkv:2 * nkv].astype(BF16)
    slabs = zqi0_ref.shape[1] // LANES
    for j in range(IDX_HEADS * IDX_DIM // LANES):
        src = zqi0_ref if j < slabs else zqi1_ref
        x = src[:, (j % slabs) * LANES:(j % slabs + 1) * LANES]
        qi_ref[:, j * LANES:(j + 1) * LANES] = _rope_lanes(x, ci, si, half_i, IDX_DIM).astype(BF16)
    slab = _rope_lanes(zkw_ref[:, 0:LANES], ci, si, half_i, IDX_DIM)
    ki_ref[...] = slab[:, 0:IDX_DIM]
    lane = lax.broadcasted_iota(jnp.int32, slab.shape, 1)
    lo = jnp.where(lane < IDX_DIM, slab, 0.0)
    kilo_ref[...] = lo.astype(BF16)
    kihi_ref[...] = pltpu.roll(lo, IDX_DIM, 1).astype(BF16)


def _prep_a(za, zb, cq, sq, ci, si, gq, gk):
    m = za.shape[0]
    tm = _pick_tile(m, (256, 128, 96, 64, 32, 16))
    row = lambda w, c0: pl.BlockSpec((tm, w), lambda i: (i, c0 // w))
    full = lambda w: pl.BlockSpec((tm, w), lambda i: (i, 0))
    par = pl.BlockSpec((1, LANES), lambda i: (0, 0))
    nq, nkv = A_HEADS * HEAD_DIM, A_KV_HEADS * HEAD_DIM
    nqi = IDX_HEADS * IDX_DIM // 2
    sds = lambda w, dt: jax.ShapeDtypeStruct((m, w), dt)
    return pl.pallas_call(
        _prep_a_kernel,
        grid=(m // tm,),
        in_specs=[row(nq, AB_Q), row(nqi, AB_QI), row(nqi, AB_QI + nqi), row(2 * nkv, AB_K),
                  row(N_ALIGN, AB_KW), full(LANES), full(LANES), full(LANES), full(LANES), par, par],
        out_specs=[full(nq), full(nq), full(nkv), full(nkv), full(nkv), full(IDX_DIM), full(LANES), full(LANES)],
        out_shape=[sds(nq, BF16), sds(nq, BF16), sds(nkv, F32), sds(nkv, BF16), sds(nkv, BF16),
                   sds(IDX_DIM, F32), sds(LANES, BF16), sds(LANES, BF16)],
        compiler_params=_cparams(("parallel",)),
        name="prep_a",
    )(za, za, za, za, zb, cq, sq, ci, si, gq, gk)


def _rope_tables(pos, head_dim):
    r = head_dim // ROT_FRACTION
    half = r // 2
    inv = 1.0 / (ROPE_THETA ** (jnp.arange(half, dtype=F32) / half))
    ang = pos.astype(F32)[:, None] * inv[None, :]
    cos, sin = jnp.cos(ang), jnp.sin(ang)
    lane = np.arange(LANES) % head_dim
    idx = lane % half
    in_rot = jnp.asarray(lane < r)[None, :]
    first = jnp.asarray(lane < half)[None, :]
    c = jnp.where(in_rot, cos[:, idx], 1.0)
    s = jnp.where(first, -sin[:, idx], jnp.where(in_rot, sin[:, idx], 0.0))
    return c, s


def _dsa_kernel(qi_ref, kilo_ref, kihi_ref, wi_ref, q_ref, k_ref, v_ref, o_ref, key_scr, bias_scr, *,
                q_pos0, l_valid, n_sel):
    tq = q_ref.shape[0]
    lp = k_ref.shape[0]
    dn = (((1,), (1,)), ((), ()))

    kilo = kilo_ref[...]
    kihi = kihi_ref[...]
    wi = wi_ref[...] * ((IDX_HEADS * IDX_DIM) ** -0.5)
    score = jnp.zeros((tq, lp), F32)
    for j in range(IDX_HEADS // 2):
        qi2 = qi_ref[:, j * LANES:(j + 1) * LANES]
        s_lo = lax.dot_general(qi2, kilo, dn, preferred_element_type=F32)
        s_hi = lax.dot_general(qi2, kihi, dn, preferred_element_type=F32)
        c = IDX_DIM + 2 * j
        score = score + jnp.maximum(s_lo, 0.0) * wi[:, c:c + 1]
        score = score + jnp.maximum(s_hi, 0.0) * wi[:, c + 1:c + 2]

    q_pos = q_pos0 + pl.program_id(1) * tq + lax.broadcasted_iota(jnp.int32, (tq, lp), 0)
    k_pos = lax.broadcasted_iota(jnp.int32, (tq, lp), 1)
    shift = CHUNK.bit_length() - 1
    adm = jnp.logical_and(jnp.right_shift(k_pos, shift) <= jnp.right_shift(q_pos, shift), k_pos < l_valid)
    bits = pltpu.bitcast(score, jnp.int32)
    key = jnp.where(bits < 0, jnp.bitwise_xor(bits, jnp.int32(0x7FFFFFFF)), bits)
    key_scr[...] = jnp.where(adm, key, jnp.int32(INT_MIN))

    def count_ge(t):
        return jnp.sum(jnp.where(key_scr[...] >= t, 1.0, 0.0), axis=-1, keepdims=True)

    zero = jnp.zeros((tq, 1), jnp.int32)
    t0 = jnp.where(count_ge(zero) >= n_sel, zero, jnp.int32(INT_MIN))

    def body(i, t):
        cand = jnp.bitwise_or(t, jnp.left_shift(jnp.int32(1), 30 - i))
        return jnp.where(count_ge(cand) >= n_sel, cand, t)

    thr = lax.fori_loop(0, 31, body, t0)
    keys = key_scr[...]
    bias_scr[...] = jnp.where(jnp.logical_and(keys >= thr, adm), 0.0, -jnp.inf)

    need = n_sel - jnp.sum(jnp.where(keys > thr, 1.0, 0.0), axis=-1, keepdims=True)
    n_tied = jnp.sum(jnp.where(keys == thr, 1.0, 0.0), axis=-1, keepdims=True)
    finite_thr = thr > jnp.int32(INT_MIN)
    over = jnp.logical_and(n_tied > need, finite_thr)

    @pl.when(jnp.max(jnp.where(over, 1.0, 0.0)) > 0.0)
    def _():
        tri = jnp.where(lax.broadcasted_iota(jnp.int32, (LANES, LANES), 0)
                        <= lax.broadcasted_iota(jnp.int32, (LANES, LANES), 1), 1.0, 0.0).astype(BF16)
        carry = jnp.zeros((tq, 1), F32)
        for c in range(lp // LANES):
            cols = slice(c * LANES, (c + 1) * LANES)
            k_c = key_scr[:, cols]
            tied = jnp.where(k_c == thr, 1.0, 0.0)
            rank = jnp.dot(tied.astype(BF16), tri, preferred_element_type=F32) + carry
            keep_tie = jnp.logical_and(jnp.logical_and(k_c == thr, rank <= need), finite_thr)
            bias_scr[:, cols] = jnp.where(jnp.logical_or(k_c > thr, keep_tie), 0.0, -jnp.inf)
            carry = carry + jnp.sum(tied, axis=-1, keepdims=True)

    group = A_HEADS // A_KV_HEADS
    for kvh in range(A_KV_HEADS):
        kk = k_ref[:, kvh * HEAD_DIM:(kvh + 1) * HEAD_DIM]
        vv = v_ref[:, kvh * HEAD_DIM:(kvh + 1) * HEAD_DIM]
        for g in range(group):
            cols = slice((kvh * group + g) * HEAD_DIM, (kvh * group + g + 1) * HEAD_DIM)
            s = lax.dot_general(q_ref[:, cols], kk, dn, preferred_element_type=F32) + bias_scr[...]
            m = jnp.max(s, axis=-1, keepdims=True)
            p = jnp.exp(s - m)
            l = jnp.sum(p, axis=-1, keepdims=True)
            o = jnp.dot(p.astype(BF16), vv, preferred_element_type=F32) / l
            o_ref[:, cols] = o.astype(o_ref.dtype)


def _dsa_attention(qi, ki_lo, ki_hi, z, q, k, v, *, n_batch, t, tq, row0, q_stride, lp, q_pos0, l_valid, n_sel):
    nq = t // tq
    qrow = lambda bi, i: (row0 + bi * q_stride) // tq + i
    qmap = lambda bi, i: (qrow(bi, i), 0)
    kmap = lambda bi, i: (bi, 0, 0)
    return pl.pallas_call(
        functools.partial(_dsa_kernel, q_pos0=q_pos0, l_valid=l_valid, n_sel=n_sel),
        grid=(n_batch, nq),
        in_specs=[pl.BlockSpec((tq, qi.shape[1]), qmap),
                  pl.BlockSpec((None, lp, LANES), kmap),
                  pl.BlockSpec((None, lp, LANES), kmap),
                  pl.BlockSpec((tq, LANES), lambda bi, i: (qrow(bi, i), AB_KW // LANES)),
                  pl.BlockSpec((tq, q.shape[1]), qmap),
                  pl.BlockSpec((None, lp, k.shape[2]), kmap),
                  pl.BlockSpec((None, lp, v.shape[2]), kmap)],
        out_specs=pl.BlockSpec((tq, q.shape[1]), lambda bi, i: (bi * nq + i, 0)),
        out_shape=jax.ShapeDtypeStruct((n_batch * t, q.shape[1]), BF16),
        scratch_shapes=[pltpu.VMEM((tq, lp), jnp.int32), pltpu.VMEM((tq, lp), F32)],
        compiler_params=_cparams(("parallel", "arbitrary")),
        name="dsa_attention",
    )(qi, ki_lo, ki_hi, z, q, k, v)


def _conv_kernel(bg_ref, cg_ref, u_ref, st_ref, w_ref, y_ref, ns_ref):
    t = u_ref.shape[0]
    u = cg_ref[...] * u_ref[...]
    st = st_ref[...]
    row = lax.broadcasted_iota(jnp.int32, u.shape, 0)
    u1 = jnp.where(row == 0, st[1:2, :], pltpu.roll(u, 1, 0))
    u2 = jnp.where(row == 0, st[0:1, :], jnp.where(row == 1, st[1:2, :], pltpu.roll(u, 2, 0)))
    w = w_ref[...]
    y = u2 * w[0:1, :] + u1 * w[1:2, :] + u * w[2:3, :]
    y_ref[...] = (bg_ref[...] * y).astype(y_ref.dtype)
    ns_ref[...] = u[t - (CONV_W - 1):t, :]


def _short_conv(z, state, w, *, n_batch, t, row0):
    tc = 256
    nb = row0 // t
    col = lambda c0: pl.BlockSpec((t, tc), lambda bi, j: (nb + bi, c0 // tc + j))
    return pl.pallas_call(
        _conv_kernel,
        grid=(n_batch, B_WIDTH // tc),
        in_specs=[col(AB_BG), col(AB_CG), col(AB_BIN),
                  pl.BlockSpec((None, CONV_W - 1, tc), lambda bi, j: (bi, 0, j)),
                  pl.BlockSpec((CONV_W, tc), lambda bi, j: (0, j))],
        out_specs=[pl.BlockSpec((t, tc), lambda bi, j: (bi, j)),
                   pl.BlockSpec((None, CONV_W - 1, tc), lambda bi, j: (bi, 0, j))],
        out_shape=[jax.ShapeDtypeStruct((n_batch * t, B_WIDTH), BF16),
                   jax.ShapeDtypeStruct((n_batch, CONV_W - 1, B_WIDTH), F32)],
        compiler_params=_cparams(("parallel", "parallel")),
        name="short_conv",
    )(z, z, z, state, w)


def _cumsum_rows(x):
    n = x.shape[0]
    row = lax.broadcasted_iota(jnp.int32, x.shape, 0)
    s = 1
    while s < n:
        x = x + jnp.where(row >= s, pltpu.roll(x, s, 0), 0.0)
        s *= 2
    return x


def _bcast_rows(x, period, offset):
    n, w = x.shape
    return jnp.concatenate(
        [jnp.broadcast_to(x[b * period + offset:b * period + offset + 1, :], (period, w)) for b in range(n // period)],
        axis=0)


def _hgrn_kernel(q_ref, f_ref, v_ref, g_ref, lb_ref, gain_ref, s0_ref, o_ref, s_out_ref, st_scr, *, tc):
    step = pl.program_id(2)
    tb = q_ref.shape[0]
    nt = (((1,), (1,)), ((), ()))
    tn = (((0,), (0,)), ((), ()))

    @pl.when(step == 0)
    def _():
        st_scr[...] = s0_ref[...].T

    lbv = lb_ref[...]
    row = lax.broadcasted_iota(jnp.int32, (tc, C_DK), 0)
    trow = lax.broadcasted_iota(jnp.int32, (tc, tc), 0)
    tcol = lax.broadcasted_iota(jnp.int32, (tc, tc), 1)
    levels = [m for m in (32, 16, 8) if 2 * m <= tc]

    def chunk(c, carry):
        rows = pl.ds(pl.multiple_of(c * tc, tc), tc)
        f = lbv + (1.0 - lbv) * jax.nn.sigmoid(f_ref[rows, :])
        kk = 1.0 - f
        a = _cumsum_rows(jnp.log(f))
        qr = q_ref[rows, :]
        q = qr * jax.nn.sigmoid(qr)
        v = v_ref[rows, :].astype(BF16)
        st = st_scr[...]

        o = lax.dot_general((q * jnp.exp(a)).astype(BF16), st.astype(BF16), nt, preferred_element_type=F32)

        sc = jnp.zeros((tc, tc), F32)
        for m in levels:
            ref = _bcast_rows(a, 2 * m, m - 1)
            upper = jnp.bitwise_and(row, 2 * m - 1) >= m
            e = jnp.exp(jnp.where(upper, a - ref, ref - a))
            qm = jnp.where(upper, q * e, 0.0).astype(BF16)
            km = jnp.where(upper, 0.0, kk * e).astype(BF16)
            sm = lax.dot_general(qm, km, nt, preferred_element_type=F32)
            shift = (2 * m).bit_length() - 1
            same = jnp.right_shift(trow, shift) == jnp.right_shift(tcol, shift)
            sc = sc + jnp.where(same, sm, 0.0)
        for s in range(SUBLANES):
            ok = jnp.bitwise_and(row, SUBLANES - 1) >= s
            d = jnp.where(ok, a - _bcast_rows(a, SUBLANES, s), 0.0)
            w = q * _bcast_rows(kk, SUBLANES, s) * jnp.exp(d)
            col = jnp.sum(w, axis=-1, keepdims=True)
            here = jnp.logical_and(tcol == jnp.bitwise_and(trow, -SUBLANES) + s,
                                   jnp.bitwise_and(trow, SUBLANES - 1) >= s)
            sc = sc + jnp.where(here, col, 0.0)

        o = o + jnp.dot(sc.astype(BF16), v, preferred_element_type=F32)

        a_last = a[tc - 1:tc, :]
        kd = (kk * jnp.exp(a_last - a)).astype(BF16)
        st_scr[...] = st * jnp.exp(a_last) + lax.dot_general(v, kd, tn, preferred_element_type=F32)

        ms = jnp.mean(o * o, axis=-1, keepdims=True)
        gr = g_ref[rows, :]
        y = o * lax.rsqrt(ms + NORM_EPS) * gain_ref[...] * (gr * jax.nn.sigmoid(gr))
        o_ref[rows, :] = y.astype(o_ref.dtype)
        return carry

    lax.fori_loop(0, tb // tc, chunk, 0, unroll=min(HGRN_UNROLL, tb // tc))

    @pl.when(step == pl.num_programs(2) - 1)
    def _():
        s_out_ref[...] = st_scr[...].T


def _hgrn2(z, lb, gain, s0, *, n_batch, t, row0):
    tc = min(CHUNK, t)
    tb = min(512, t)
    nt = t // tb
    hb = CD_W // C_DK
    xspec = lambda seg: pl.BlockSpec((tb, C_DK), lambda bi, h, i: (row0 // tb + bi * nt + i, seg * hb + h))
    smap = lambda bi, h, i: (bi, h, 0, 0)
    return pl.pallas_call(
        functools.partial(_hgrn_kernel, tc=tc),
        grid=(n_batch, C_HEADS, nt),
        in_specs=[xspec(0), xspec(1), xspec(2), xspec(3),
                  pl.BlockSpec((1, C_DK), lambda bi, h, i: (0, h)),
                  pl.BlockSpec((1, C_DV), lambda bi, h, i: (0, 0)),
                  pl.BlockSpec((None, None, C_DK, C_DV), smap)],
        out_specs=[pl.BlockSpec((tb, C_DV), lambda bi, h, i: (bi * nt + i, h)),
                   pl.BlockSpec((None, None, C_DK, C_DV), smap)],
        out_shape=[jax.ShapeDtypeStruct((n_batch * t, CD_W), BF16), jax.ShapeDtypeStruct(s0.shape, F32)],
        scratch_shapes=[pltpu.VMEM((C_DV, C_DK), F32)],
        compiler_params=_cparams(("parallel", "parallel", "arbitrary")),
        name="hgrn2",
    )(z, z, z, z, lb, gain, s0)


def _cumsum_kernel(x_ref, o_ref):
    o_ref[...] = _cumsum_rows(x_ref[...])


def _cumsum_time(x):
    b, l, w = x.shape
    return pl.pallas_call(
        _cumsum_kernel,
        grid=(b,),
        in_specs=[pl.BlockSpec((None, l, w), lambda i: (i, 0, 0))],
        out_specs=pl.BlockSpec((None, l, w), lambda i: (i, 0, 0)),
        out_shape=jax.ShapeDtypeStruct(x.shape, F32),
        compiler_params=_cparams(("parallel",)),
        name="cumsum_time",
    )(x)


def _prep_d_kernel(zq_ref, zk_ref, zf_ref, gq_ref, gk_ref, fb_ref, q_ref, k_ref, kb_ref, lf_ref):
    for h in range(D_HEADS):
        cols = slice(h * HEAD_DIM, (h + 1) * HEAD_DIM)
        q_ref[:, cols] = (_head_rms(zq_ref[:, cols], gq_ref[...]) * (HEAD_DIM ** -0.5)).astype(BF16)
        y = _head_rms(zk_ref[:, cols], gk_ref[...])
        k_ref[:, cols] = y
        kb_ref[:, cols] = y.astype(BF16)
    x = zf_ref[:, 0:LANES] + fb_ref[...]
    lf_ref[...] = jnp.minimum(x, 0.0) - jnp.log(1.0 + jnp.exp(-jnp.abs(x)))


def _prep_d(za, zb, gq, gk, fb):
    m = za.shape[0]
    tm = _pick_tile(m, (256, 128, 96, 64, 32, 16))
    row = lambda w, c: pl.BlockSpec((tm, w), lambda i: (i, c))
    full = lambda w: pl.BlockSpec((tm, w), lambda i: (i, 0))
    par = pl.BlockSpec((1, LANES), lambda i: (0, 0))
    sds = lambda w, dt: jax.ShapeDtypeStruct((m, w), dt)
    return pl.pallas_call(
        _prep_d_kernel,
        grid=(m // tm,),
        in_specs=[row(CD_W, 4), row(CD_W, 5), row(N_ALIGN, CD_FD // N_ALIGN), par, par, par],
        out_specs=[full(CD_W), full(CD_W), full(CD_W), full(LANES)],
        out_shape=[sds(CD_W, BF16), sds(CD_W, F32), sds(CD_W, BF16), sds(LANES, F32)],
        compiler_params=_cparams(("parallel",)),
        name="prep_d",
    )(za, za, zb, gq, gk, fb)


def _fox_kernel(q_ref, k_ref, v_ref, cq_ref, ck_ref, g_ref, o_ref, *, tq):
    t = q_ref.shape[0]
    dn = (((1,), (1,)), ((), ()))
    row = lax.broadcasted_iota(jnp.int32, (tq, tq), 0)
    col = lax.broadcasted_iota(jnp.int32, (tq, tq), 1)
    for qi in range(t // tq):
        lo, hi = qi * tq, (qi + 1) * tq
        q = q_ref[lo:hi, :]
        cq = cq_ref[lo:hi, :]
        s_d = lax.dot_general(q, k_ref[lo:hi, :], dn, preferred_element_type=F32) + cq - ck_ref[:, lo:hi]
        s_d = jnp.where(col <= row, s_d, -jnp.inf)
        m = jnp.max(s_d, axis=-1, keepdims=True)
        if qi:
            s_o = lax.dot_general(q, k_ref[0:lo, :], dn, preferred_element_type=F32) + cq - ck_ref[:, 0:lo]
            m = jnp.maximum(m, jnp.max(s_o, axis=-1, keepdims=True))
        p_d = jnp.exp(s_d - m)
        l = jnp.sum(p_d, axis=-1, keepdims=True)
        acc = jnp.dot(p_d.astype(BF16), v_ref[lo:hi, :].astype(BF16), preferred_element_type=F32)
        if qi:
            p_o = jnp.exp(s_o - m)
            l = l + jnp.sum(p_o, axis=-1, keepdims=True)
            acc = acc + jnp.dot(p_o.astype(BF16), v_ref[0:lo, :].astype(BF16), preferred_element_type=F32)
        o = acc / l * jax.nn.sigmoid(g_ref[lo:hi, :])
        o_ref[lo:hi, :] = o.astype(o_ref.dtype)


def _fox_prefill(q, k, za, zb, cum_col, cum_row, *, n_batch, t):
    hb = CD_W // HEAD_DIM
    tq = min(FOX_TQ, t)
    qmap = lambda bi, h: (bi, h)
    return pl.pallas_call(
        functools.partial(_fox_kernel, tq=tq),
        grid=(n_batch, D_HEADS),
        in_specs=[pl.BlockSpec((t, HEAD_DIM), qmap),
                  pl.BlockSpec((t, HEAD_DIM), qmap),
                  pl.BlockSpec((t, HEAD_DIM), lambda bi, h: (bi, 6 * hb + h)),
                  pl.BlockSpec((None, None, t, 1), lambda bi, h: (bi, h, 0, 0)),
                  pl.BlockSpec((None, None, 1, t), lambda bi, h: (bi, h, 0, 0)),
                  pl.BlockSpec((t, HEAD_DIM), lambda bi, h: (bi, h))],
        out_specs=pl.BlockSpec((t, HEAD_DIM), qmap),
        out_shape=jax.ShapeDtypeStruct((n_batch * t, CD_W), BF16),
        compiler_params=_cparams(("parallel", "parallel")),
        name="fox_prefill",
    )(q, k, za, cum_col, cum_row, zb)


def _fox_dec_kernel(q_ref, kc_ref, vc_ref, kn_ref, vn_ref, cq_ref, ckc_ref, ckn_ref, g_ref, o_ref,
                    m_scr, l_scr, acc_scr):
    j = pl.program_id(1)
    rows = q_ref.shape[0]
    hmask = D_HEADS - 1
    hshift = D_HEADS.bit_length() - 1
    dn = (((1,), (1,)), ((), ()))
    q = q_ref[...]
    cq = cq_ref[...]

    def scores(k, ck, causal):
        s = lax.dot_general(q, k, dn, preferred_element_type=F32) + cq - ck
        r = lax.broadcasted_iota(jnp.int32, s.shape, 0)
        c = lax.broadcasted_iota(jnp.int32, s.shape, 1)
        ok = jnp.bitwise_and(r, hmask) == jnp.bitwise_and(c, hmask)
        if causal:
            ok = jnp.logical_and(ok, jnp.right_shift(c, hshift) <= jnp.right_shift(r, hshift))
        return jnp.where(ok, s, -jnp.inf)

    @pl.when(j == 0)
    def _():
        s = scores(kn_ref[...], ckn_ref[...], True)
        m = jnp.max(s, axis=-1, keepdims=True)
        p = jnp.exp(s - m)
        m_scr[...] = m
        l_scr[...] = jnp.sum(p, axis=-1, keepdims=True)
        acc_scr[...] = jnp.dot(p.astype(BF16), vn_ref[...].astype(BF16), preferred_element_type=F32)

    s = scores(kc_ref[...].astype(BF16), ckc_ref[...], False)
    m_prev = m_scr[...]
    m_new = jnp.maximum(m_prev, jnp.max(s, axis=-1, keepdims=True))
    alpha = jnp.exp(m_prev - m_new)
    p = jnp.exp(s - m_new)
    l_scr[...] = alpha * l_scr[...] + jnp.sum(p, axis=-1, keepdims=True)
    acc_scr[...] = alpha * acc_scr[...] + jnp.dot(p.astype(BF16), vc_ref[...].astype(BF16),
                                                  preferred_element_type=F32)
    m_scr[...] = m_new

    @pl.when(j == pl.num_programs(1) - 1)
    def _():
        o = acc_scr[...] / l_scr[...] * jax.nn.sigmoid(g_ref[...])
        o_ref[...] = o.astype(o_ref.dtype)


def _fox_decode(q, k_new, v_new, g, k_cache, v_cache, cq_col, ck_cache_row, ck_new_row):
    b, rows, _ = q.shape
    pc = k_cache.shape[1]
    blk = min(FOX_DEC_POS * D_HEADS, pc)
    new = pl.BlockSpec((None, rows, HEAD_DIM), lambda bi, j: (bi, 0, 0))
    cache = pl.BlockSpec((None, blk, HEAD_DIM), lambda bi, j: (bi, j, 0))
    return pl.pallas_call(
        _fox_dec_kernel,
        grid=(b, pc // blk),
        in_specs=[new, cache, cache, new, new,
                  pl.BlockSpec((None, rows, 1), lambda bi, j: (bi, 0, 0)),
                  pl.BlockSpec((None, 1, blk), lambda bi, j: (bi, 0, j)),
                  pl.BlockSpec((None, 1, rows), lambda bi, j: (bi, 0, 0)),
                  new],
        out_specs=new,
        out_shape=jax.ShapeDtypeStruct((b, rows, HEAD_DIM), BF16),
        scratch_shapes=[pltpu.VMEM((rows, 1), F32), pltpu.VMEM((rows, 1), F32), pltpu.VMEM((rows, HEAD_DIM), F32)],
        compiler_params=_cparams(("parallel", "arbitrary")),
        name="fox_decode",
    )(q, k_cache, v_cache, k_new, v_new, cq_col, ck_cache_row, ck_new_row, g)


def _pad_cols(w, n):
    return jnp.pad(w, ((0, 0), (0, n - w.shape[1])))


def _ab_tail_weight(w):
    o = np.cumsum((0,) + AB_SIZES)
    return jnp.concatenate([w[:, o[6]:].astype(BF16), _pad_cols(w[:, o[4]:o[6]].astype(BF16), N_ALIGN)], axis=1)


def _cd_tail_weight(w):
    o = np.cumsum((0,) + CD_SIZES)
    return jnp.concatenate([w[:, o[8]:].astype(BF16), _pad_cols(w[:, o[7]:o[8]].astype(BF16), N_ALIGN)], axis=1)


def _expand_groups(mod_p, mod_s, seq_p):
    return jnp.concatenate([jnp.repeat(mod_p, seq_p // ROW_GROUP, axis=0), mod_s], axis=0)


def _lane_row(v):
    return jnp.pad(v.astype(F32), (0, LANES - v.shape[0]))[None, :]


def kernel(x_prompt, x_sample, c_prompt, c_sample, cache_a_k, cache_a_v, cache_a_idx_k, state_b_conv, state_c_s, cache_d_k, cache_d_v, cache_d_logf, ada_w, ada_b, norm_mix, norm_ffn, ab_w_in, ab_w_out, a_q_norm, a_k_norm, b_conv_w, cd_w_in, cd_w_out, c_lb, c_o_norm, d_q_norm, d_k_norm, d_f_bias, ffn_w_gate, ffn_w_up, ffn_w_down):
    bp, tp, d = x_prompt.shape
    bs, ts, _ = x_sample.shape
    past = cache_a_k.shape[2]
    depth = ada_w.shape[0]
    assert ts == ROW_GROUP and tp % ROW_GROUP == 0 and d == D_MODEL
    mp, ms = bp * tp, bs * ts

    x = jnp.concatenate([x_prompt.reshape(mp, d), x_sample.reshape(ms, d)], axis=0)
    c_all = jnp.concatenate([c_prompt, c_sample], axis=0)
    c_rows = -(-c_all.shape[0] // SUBLANES) * SUBLANES
    c_pad = jnp.pad(c_all, ((0, c_rows - c_all.shape[0]), (0, 0)))

    lb_all = jnp.cumsum(jax.nn.softmax(c_lb.astype(F32), axis=0), axis=0)
    lb_all = lb_all - lb_all[0]

    pos = jnp.concatenate([jnp.tile(jnp.arange(tp), bp), past + jnp.tile(jnp.arange(ts), bs)])
    cos_q, sin_q = _rope_tables(pos, HEAD_DIM)
    cos_i, sin_i = _rope_tables(pos, IDX_DIM)

    outs_p, outs_s = {}, {}
    mod_all = _ada_mod(c_pad, ada_w, ada_b[:, None, :])
    w_down = ffn_w_down.astype(BF16)
    SH1, SC1, G1, SH2, SC2, G2 = range(6)
    for l in range(depth):
        g16 = _expand_groups(mod_all[l, :bp], mod_all[l, bp:bp + bs], tp)
        h = _norm_mod(x, norm_mix[l][None, :], g16, SC1, SH1)
        if l % 2 == 0:
            e = l // 2
            za = _matmul_head(h, ab_w_in, e, AB_HEAD)
            z = _matmul(h, _ab_tail_weight(ab_w_in[e]))
            q_b, qi_b, k_f, k_b, v_b, ki_f, ki_lo, ki_hi = _prep_a(
                za, z, cos_q, sin_q, cos_i, sin_i, a_q_norm[e][None, :], a_k_norm[e][None, :])
            v_f = za[:, AB_V:AB_V + A_KV_HEADS * HEAD_DIM]

            keys_p = [a[:mp].reshape(bp, tp, -1) for a in (ki_lo, ki_hi, k_b, v_b)]
            grp = DSA_GROUP if tp % DSA_GROUP == 0 else tp
            ya_groups = []
            for g0 in range(0, tp, grp):
                o = _dsa_attention(qi_b, keys_p[0], keys_p[1], z, q_b, keys_p[2], keys_p[3], n_batch=bp, t=grp,
                                   tq=min(Q_BLOCK, grp), row0=g0, q_stride=tp, lp=g0 + grp, q_pos0=g0,
                                   l_valid=g0 + grp, n_sel=min(TOPK_MAX, tp // 4))
                ya_groups.append(o.reshape(bp, grp, -1))
            ya_p = jnp.concatenate(ya_groups, axis=1).reshape(mp, -1)
            ls = past + ts
            lp = -(-ls // LANES) * LANES

            def with_cache(cache, new_rows):
                cache = cache.reshape(bs, past, -1).astype(BF16)
                new_rows = new_rows[mp:].reshape(bs, ts, -1)
                full = jnp.concatenate([cache, new_rows], axis=1)
                return jnp.pad(full, ((0, 0), (0, lp - ls), (0, 0)))

            zi = jnp.zeros_like(cache_a_idx_k[e])
            ya_s = _dsa_attention(
                qi_b, with_cache(jnp.concatenate([cache_a_idx_k[e], zi], -1), ki_lo),
                with_cache(jnp.concatenate([zi, cache_a_idx_k[e]], -1), ki_hi), z, q_b,
                with_cache(cache_a_k[e], k_b), with_cache(cache_a_v[e], v_b),
                n_batch=bs, t=ts, tq=ts, row0=mp, q_stride=ts, lp=lp, q_pos0=past, l_valid=ls,
                n_sel=min(TOPK_MAX, ls // 4))

            yb_p, conv_p = _short_conv(z, jnp.zeros((bp, CONV_W - 1, B_WIDTH), F32), b_conv_w[e],
                                       n_batch=bp, t=tp, row0=0)
            yb_s, conv_s = _short_conv(z, state_b_conv[e], b_conv_w[e], n_batch=bs, t=ts, row0=mp)
            x = _mix_out(jnp.concatenate([ya_p, ya_s], axis=0), jnp.concatenate([yb_p, yb_s], axis=0),
                         ab_w_out[e].astype(BF16), x, g16, G1)

            for od, sl, bn, t, conv_new in ((outs_p, slice(0, mp), bp, tp, conv_p),
                                            (outs_s, slice(mp, mp + ms), bs, ts, conv_s)):
                od.setdefault("a_k", []).append(k_f[sl].reshape(bn, t, A_KV_HEADS, HEAD_DIM))
                od.setdefault("a_v", []).append(v_f[sl].reshape(bn, t, A_KV_HEADS, HEAD_DIM))
                od.setdefault("a_ik", []).append(ki_f[sl].reshape(bn, t, IDX_DIM))
                od.setdefault("b_conv", []).append(conv_new)
        else:
            od_ = l // 2
            za = _matmul_head(h, cd_w_in, od_, CD_HEAD)
            z = _matmul(h, _cd_tail_weight(cd_w_in[od_]))
            q_b, k_f, k_b, lf_slab = _prep_d(za, z, d_q_norm[od_][None, :], d_k_norm[od_][None, :],
                                             _lane_row(d_f_bias[od_]))
            logf = lf_slab[:, :D_HEADS]
            v_f = za[:, 6 * CD_W:7 * CD_W]
            lb = lb_all[l][None, :]
            gain = c_o_norm[od_][None, :]

            yc_p, s_p = _hgrn2(za, lb, gain, jnp.zeros((bp, C_HEADS, C_DK, C_DV), F32), n_batch=bp, t=tp, row0=0)
            yc_s, s_s = _hgrn2(za, lb, gain, state_c_s[od_], n_batch=bs, t=ts, row0=mp)

            cum_p = _cumsum_time(lf_slab[:mp].reshape(bp, tp, LANES))[:, :, :D_HEADS].transpose(0, 2, 1)
            yd_p = _fox_prefill(q_b, k_b, za, z, cum_p[..., None], cum_p[:, :, None, :], n_batch=bp, t=tp)
            lf_cache = jnp.pad(cache_d_logf[od_], ((0, 0), (0, 0), (0, LANES - D_HEADS)))
            cum_s = _cumsum_time(jnp.concatenate([lf_cache, lf_slab[mp:].reshape(bs, ts, LANES)], axis=1))
            cum_s = cum_s[:, :, :D_HEADS]
            by_head = lambda a: a.reshape(bs, -1, HEAD_DIM)
            yd_s = _fox_decode(by_head(q_b[mp:]), by_head(k_b[mp:]), by_head(v_f[mp:]), by_head(z[mp:, :CD_W]),
                               by_head(cache_d_k[od_]), by_head(cache_d_v[od_]),
                               cum_s[:, past:].reshape(bs, ts * D_HEADS, 1),
                               cum_s[:, :past].reshape(bs, 1, past * D_HEADS),
                               cum_s[:, past:].reshape(bs, 1, ts * D_HEADS)).reshape(ms, CD_W)
            x = _mix_out(jnp.concatenate([yc_p, yc_s], axis=0), jnp.concatenate([yd_p, yd_s], axis=0),
                         cd_w_out[od_].astype(BF16), x, g16, G1)

            for od, sl, bn, t, s_new in ((outs_p, slice(0, mp), bp, tp, s_p),
                                         (outs_s, slice(mp, mp + ms), bs, ts, s_s)):
                od.setdefault("c_s", []).append(s_new)
                od.setdefault("d_k", []).append(k_f[sl].reshape(bn, t, D_HEADS, HEAD_DIM))
                od.setdefault("d_v", []).append(v_f[sl].reshape(bn, t, D_HEADS, HEAD_DIM))
                od.setdefault("d_lf", []).append(logf[sl].reshape(bn, t, D_HEADS))
        h2 = _norm_mod(x, norm_ffn[l][None, :], g16, SC2, SH2)
        hid = _ffn_gate_up(h2, ffn_w_gate, ffn_w_up, l)
        x = _ffn_down(hid, w_down, l, x, g16, G2)

    y_prompt = x[:mp].reshape(bp, tp, d)
    y_sample = x[mp:].reshape(bs, ts, d)
    names = ("a_k", "a_v", "a_ik", "b_conv", "c_s", "d_k", "d_v", "d_lf")
    return ((y_prompt, y_sample) + tuple(jnp.stack(outs_p[n]) for n in names)
            + tuple(jnp.stack(outs_s[n]) for n in names))
```

```python
import functools

import numpy as np
import jax
import jax.numpy as jnp
from jax import lax
from jax.experimental import pallas as pl
from jax.experimental.pallas import tpu as pltpu

F32 = jnp.float32
BF16 = jnp.bfloat16

D_MODEL = 4096
CHUNK = 64
Q_BLOCK = 128
HEAD_DIM = 128
ROPE_THETA = 500000.0
ROT_FRACTION = 4
NORM_EPS = 1e-6
A_HEADS = D_MODEL // 256
A_KV_HEADS = A_HEADS // 4
IDX_HEADS = D_MODEL // 128
IDX_DIM = 64
TOPK_MAX = 256
B_WIDTH = D_MODEL // 2
CONV_W = 3
C_HEADS = D_MODEL // 256
C_DK = 128
C_DV = 128
D_HEADS = D_MODEL // 256
AB_SIZES = (A_HEADS * HEAD_DIM, A_KV_HEADS * HEAD_DIM, A_KV_HEADS * HEAD_DIM,
            IDX_HEADS * IDX_DIM, IDX_DIM, IDX_HEADS, B_WIDTH, B_WIDTH, B_WIDTH)
CD_SIZES = (C_HEADS * C_DK, C_HEADS * C_DK, C_HEADS * C_DV, C_HEADS * C_DV,
            D_HEADS * HEAD_DIM, D_HEADS * HEAD_DIM, D_HEADS * HEAD_DIM, D_HEADS, D_HEADS * HEAD_DIM)

LANES = 128
SUBLANES = 8
VMEM_LIMIT_BYTES = 56 * 1024 * 1024

ROW_GROUP = 16
N_ALIGN = 512
INT_MIN = -2 ** 31
HGRN_UNROLL = 4
FOX_TQ = 256
FOX_DEC_POS = 512
DSA_GROUP = 512
DSA_TQ = 256

AB_Q, AB_K, AB_V, AB_QI, AB_HEAD = 0, 2048, 2560, 3072, 5120
AB_BG, AB_CG, AB_BIN, AB_KW = 0, 2048, 4096, 6144
CD_W = 2048
CD_HEAD = 7 * CD_W
CD_FD = CD_W


def _cparams(sem):
    return pltpu.CompilerParams(dimension_semantics=sem, vmem_limit_bytes=VMEM_LIMIT_BYTES)


def _pick_tile(n, candidates):
    for c in candidates:
        if n % c == 0:
            return c
    raise ValueError(f"no tile for {n}")


M_TILES = (768, 512, 256, 128, 96, 64, 32, 16)


def _ada_kernel(c_ref, w_ref, b_ref, o_ref):
    c = c_ref[...]
    a = (c * jax.nn.sigmoid(c)).astype(BF16)
    o_ref[...] = jnp.dot(a, w_ref[...].astype(BF16), preferred_element_type=F32) + b_ref[...]


def _ada_mod(c, w, b):
    r, d = c.shape
    depth, _, n = w.shape
    tn = _pick_tile(n, (512, 256, 128))
    return pl.pallas_call(
        _ada_kernel,
        grid=(depth, n // tn),
        in_specs=[pl.BlockSpec((r, d), lambda l, j: (0, 0)),
                  pl.BlockSpec((None, d, tn), lambda l, j: (l, 0, j)),
                  pl.BlockSpec((None, 1, tn), lambda l, j: (l, 0, j))],
        out_specs=pl.BlockSpec((None, r, tn), lambda l, j: (l, 0, j)),
        out_shape=jax.ShapeDtypeStruct((depth, r, n), F32),
        compiler_params=_cparams(("parallel", "parallel")),
        name="ada_mod",
    )(c, w, b)


def _norm_mod_kernel(x_ref, gain_ref, sc_ref, sh_ref, o_ref, *, groups):
    gain = gain_ref[...]
    for g in range(groups):
        rows = slice(g * ROW_GROUP, (g + 1) * ROW_GROUP)
        x = x_ref[rows, :]
        ms = jnp.mean(x * x, axis=-1, keepdims=True)
        y = x * lax.rsqrt(ms + NORM_EPS) * gain
        o_ref[rows, :] = (y * (1.0 + sc_ref[g:g + 1, :]) + sh_ref[g:g + 1, :]).astype(o_ref.dtype)


def _norm_mod(x, gain, g16, sc_seg, sh_seg):
    m, d = x.shape
    tm = _pick_tile(m, (384,) + M_TILES)
    groups = tm // ROW_GROUP
    return pl.pallas_call(
        functools.partial(_norm_mod_kernel, groups=groups),
        grid=(m // tm,),
        in_specs=[pl.BlockSpec((tm, d), lambda i: (i, 0)),
                  pl.BlockSpec((1, d), lambda i: (0, 0)),
                  pl.BlockSpec((groups, d), lambda i: (i, sc_seg)),
                  pl.BlockSpec((groups, d), lambda i: (i, sh_seg))],
        out_specs=pl.BlockSpec((tm, d), lambda i: (i, 0)),
        out_shape=jax.ShapeDtypeStruct((m, d), BF16),
        compiler_params=_cparams(("parallel",)),
        name="norm_mod",
    )(x, gain, g16, g16)


def _mm_kernel(a_ref, w_ref, o_ref):
    o_ref[...] = jnp.dot(a_ref[...], w_ref[...], preferred_element_type=F32).astype(o_ref.dtype)


def _matmul(a, w):
    m, k = a.shape
    n = w.shape[1]
    tm = _pick_tile(m, M_TILES)
    tn = _pick_tile(n, (512, 256, 128))
    return pl.pallas_call(
        _mm_kernel,
        grid=(m // tm, n // tn),
        in_specs=[pl.BlockSpec((tm, k), lambda i, j: (i, 0)),
                  pl.BlockSpec((k, tn), lambda i, j: (0, j))],
        out_specs=pl.BlockSpec((tm, tn), lambda i, j: (i, j)),
        out_shape=jax.ShapeDtypeStruct((m, n), F32),
        compiler_params=_cparams(("parallel", "parallel")),
        name="proj_in",
    )(a, w)


def _mm_w32_kernel(a_ref, w_ref, o_ref):
    o_ref[...] = jnp.dot(a_ref[...], w_ref[...].astype(BF16), preferred_element_type=F32)


def _matmul_head(a, w, layer, n_cols):
    m, k = a.shape
    tm = _pick_tile(m, (1056,) + M_TILES)
    tn = _pick_tile(n_cols, (512, 256, 128))
    return pl.pallas_call(
        _mm_w32_kernel,
        grid=(m // tm, n_cols // tn),
        in_specs=[pl.BlockSpec((tm, k), lambda i, j: (i, 0)),
                  pl.BlockSpec((None, k, tn), lambda i, j: (layer, 0, j))],
        out_specs=pl.BlockSpec((tm, tn), lambda i, j: (i, j)),
        out_shape=jax.ShapeDtypeStruct((m, n_cols), F32),
        compiler_params=_cparams(("parallel", "parallel")),
        name="proj_in_head",
    )(a, w)


def _gated_residual(o_ref, x_ref, g_ref, acc, groups):
    for g in range(groups):
        rows = slice(g * ROW_GROUP, (g + 1) * ROW_GROUP)
        o_ref[rows, :] = x_ref[rows, :] + g_ref[g:g + 1, :] * acc[rows, :]


def _mix_out_kernel(a1_ref, a2_ref, w1_ref, w2_ref, x_ref, g_ref, o_ref, *, groups):
    acc = jnp.dot(a1_ref[...], w1_ref[...], preferred_element_type=F32)
    acc = acc + jnp.dot(a2_ref[...], w2_ref[...], preferred_element_type=F32)
    _gated_residual(o_ref, x_ref, g_ref, acc, groups)


def _mix_out(a1, a2, w, x, g16, gate_seg):
    m, kh = a1.shape
    n = w.shape[1]
    assert w.shape[0] == 2 * kh
    tm = _pick_tile(m, M_TILES)
    tn = _pick_tile(n, (512, 256, 128))
    groups = tm // ROW_GROUP
    return pl.pallas_call(
        functools.partial(_mix_out_kernel, groups=groups),
        grid=(m // tm, n // tn),
        in_specs=[pl.BlockSpec((tm, kh), lambda i, j: (i, 0)),
                  pl.BlockSpec((tm, kh), lambda i, j: (i, 0)),
                  pl.BlockSpec((kh, tn), lambda i, j: (0, j)),
                  pl.BlockSpec((kh, tn), lambda i, j: (1, j)),
                  pl.BlockSpec((tm, tn), lambda i, j: (i, j)),
                  pl.BlockSpec((groups, tn), lambda i, j: (i, gate_seg * (n // tn) + j))],
        out_specs=pl.BlockSpec((tm, tn), lambda i, j: (i, j)),
        out_shape=jax.ShapeDtypeStruct((m, n), F32),
        compiler_params=_cparams(("parallel", "parallel")),
        name="mix_out_resid",
    )(a1, a2, w, w, x, g16)


def _ffn_down_kernel(a_ref, w_ref, x_ref, g_ref, o_ref, acc_ref, *, nk, groups):
    k = pl.program_id(1)
    j = pl.program_id(2)
    part = jnp.dot(a_ref[...], w_ref[...], preferred_element_type=F32)
    if nk == 1:
        _gated_residual(o_ref, x_ref, g_ref, part, groups)
    else:
        @pl.when(k == 0)
        def _():
            acc_ref[j] = part

        @pl.when(jnp.logical_and(k > 0, k < nk - 1))
        def _():
            acc_ref[j] += part

        @pl.when(k == nk - 1)
        def _():
            _gated_residual(o_ref, x_ref, g_ref, acc_ref[j] + part, groups)


def _ffn_down(a, w, layer, x, g16, gate_seg):
    m, k = a.shape
    n = w.shape[2]
    tm = _pick_tile(m, M_TILES)
    tn = _pick_tile(n, (512, 256, 128))
    tk = k if k <= 4096 else _pick_tile(k, (5504, 4096, 2048, 1024))
    nk = k // tk
    groups = tm // ROW_GROUP
    last = lambda kk, j: jnp.where(kk == nk - 1, j, 0)
    return pl.pallas_call(
        functools.partial(_ffn_down_kernel, nk=nk, groups=groups),
        grid=(m // tm, nk, n // tn),
        in_specs=[pl.BlockSpec((tm, tk), lambda i, kk, j: (i, kk)),
                  pl.BlockSpec((None, tk, tn), lambda i, kk, j: (layer, kk, j)),
                  pl.BlockSpec((tm, tn), lambda i, kk, j: (i, last(kk, j))),
                  pl.BlockSpec((groups, tn), lambda i, kk, j: (i, gate_seg * (n // tn) + last(kk, j)))],
        out_specs=pl.BlockSpec((tm, tn), lambda i, kk, j: (i, last(kk, j))),
        out_shape=jax.ShapeDtypeStruct((m, n), F32),
        scratch_shapes=[pltpu.VMEM((n // tn, tm, tn), F32)],
        compiler_params=_cparams(("parallel", "arbitrary", "arbitrary")),
        name="ffn_down_resid",
    )(a, w, x, g16)


def _ffn_gate_up_kernel(a_ref, wg_ref, wu_ref, o_ref):
    a = a_ref[...]
    g = jnp.dot(a, wg_ref[...].astype(BF16), preferred_element_type=F32)
    u = jnp.dot(a, wu_ref[...].astype(BF16), preferred_element_type=F32)
    o_ref[...] = (g * jax.nn.sigmoid(g) * u).astype(o_ref.dtype)


def _ffn_gate_up(a, wg, wu, layer):
    m, k = a.shape
    n = wg.shape[2]
    tm = _pick_tile(m, (1056,) + M_TILES)
    tn = _pick_tile(n, (256, 128))
    return pl.pallas_call(
        _ffn_gate_up_kernel,
        grid=(m // tm, n // tn),
        in_specs=[pl.BlockSpec((tm, k), lambda i, j: (i, 0)),
                  pl.BlockSpec((None, k, tn), lambda i, j: (layer, 0, j)),
                  pl.BlockSpec((None, k, tn), lambda i, j: (layer, 0, j))],
        out_specs=pl.BlockSpec((tm, tn), lambda i, j: (i, j)),
        out_shape=jax.ShapeDtypeStruct((m, n), BF16),
        compiler_params=_cparams(("parallel", "parallel")),
        name="ffn_gate_up",
    )(a, wg, wu)


def _rope_lanes(y, cos, sin_signed, half, period):
    lane = lax.broadcasted_iota(jnp.int32, y.shape, 1)
    first = jnp.bitwise_and(lane, period - 1) < half
    swap = jnp.where(first, pltpu.roll(y, LANES - half, 1), pltpu.roll(y, half, 1))
    return y * cos + swap * sin_signed


def _head_rms(x, gain):
    ms = jnp.mean(x * x, axis=-1, keepdims=True)
    return x * lax.rsqrt(ms + NORM_EPS) * gain


def _prep_a_kernel(zq_ref, zqi0_ref, zqi1_ref, zkv_ref, zkw_ref, cq_ref, sq_ref, ci_ref, si_ref, gq_ref, gk_ref,
                   q_ref, qi_ref, k_ref, kb_ref, vb_ref, ki_ref, kilo_ref, kihi_ref):
    cq, sq, ci, si = cq_ref[...], sq_ref[...], ci_ref[...], si_ref[...]
    half_q = HEAD_DIM // ROT_FRACTION // 2
    half_i = IDX_DIM // ROT_FRACTION // 2
    for h in range(A_HEADS):
        cols = slice(h * HEAD_DIM, (h + 1) * HEAD_DIM)
        y = _rope_lanes(_head_rms(zq_ref[:, cols], gq_ref[...]), cq, sq, half_q, HEAD_DIM)
        q_ref[:, cols] = (y * (HEAD_DIM ** -0.5)).astype(BF16)
    for h in range(A_KV_HEADS):
        cols = slice(h * HEAD_DIM, (h + 1) * HEAD_DIM)
        y = _rope_lanes(_head_rms(zkv_ref[:, cols], gk_ref[...]), cq, sq, half_q, HEAD_DIM)
        k_ref[:, cols] = y
        kb_ref[:, cols] = y.astype(BF16)
    nkv = A_KV_HEADS * HEAD_DIM
    vb_ref[...] = zkv_ref[:, nkv:2 * nkv].astype(BF16)
    slabs = zqi0_ref.shape[1] // LANES
    for j in range(IDX_HEADS * IDX_DIM // LANES):
        src = zqi0_ref if j < slabs else zqi1_ref
        x = src[:, (j % slabs) * LANES:(j % slabs + 1) * LANES]
        qi_ref[:, j * LANES:(j + 1) * LANES] = _rope_lanes(x, ci, si, half_i, IDX_DIM).astype(BF16)
    slab = _rope_lanes(zkw_ref[:, 0:LANES], ci, si, half_i, IDX_DIM)
    ki_ref[...] = slab[:, 0:IDX_DIM]
    lane = lax.broadcasted_iota(jnp.int32, slab.shape, 1)
    lo = jnp.where(lane < IDX_DIM, slab, 0.0)
    kilo_ref[...] = lo.astype(BF16)
    kihi_ref[...] = pltpu.roll(lo, IDX_DIM, 1).astype(BF16)


def _prep_a(za, zb, cq, sq, ci, si, gq, gk):
    m = za.shape[0]
    tm = _pick_tile(m, (256, 128, 96, 64, 32, 16))
    row = lambda w, c0: pl.BlockSpec((tm, w), lambda i: (i, c0 // w))
    full = lambda w: pl.BlockSpec((tm, w), lambda i: (i, 0))
    par = pl.BlockSpec((1, LANES), lambda i: (0, 0))
    nq, nkv = A_HEADS * HEAD_DIM, A_KV_HEADS * HEAD_DIM
    nqi = IDX_HEADS * IDX_DIM // 2
    sds = lambda w, dt: jax.ShapeDtypeStruct((m, w), dt)
    return pl.pallas_call(
        _prep_a_kernel,
        grid=(m // tm,),
        in_specs=[row(nq, AB_Q), row(nqi, AB_QI), row(nqi, AB_QI + nqi), row(2 * nkv, AB_K),
                  row(N_ALIGN, AB_KW), full(LANES), full(LANES), full(LANES), full(LANES), par, par],
        out_specs=[full(nq), full(nq), full(nkv), full(nkv), full(nkv), full(IDX_DIM), full(LANES), full(LANES)],
        out_shape=[sds(nq, BF16), sds(nq, BF16), sds(nkv, F32), sds(nkv, BF16), sds(nkv, BF16),
                   sds(IDX_DIM, F32), sds(LANES, BF16), sds(LANES, BF16)],
        compiler_params=_cparams(("parallel",)),
        name="prep_a",
    )(za, za, za, za, zb, cq, sq, ci, si, gq, gk)


def _rope_tables(pos, head_dim):
    r = head_dim // ROT_FRACTION
    half = r // 2
    inv = 1.0 / (ROPE_THETA ** (jnp.arange(half, dtype=F32) / half))
    ang = pos.astype(F32)[:, None] * inv[None, :]
    cos, sin = jnp.cos(ang), jnp.sin(ang)
    lane = np.arange(LANES) % head_dim
    idx = lane % half
    in_rot = jnp.asarray(lane < r)[None, :]
    first = jnp.asarray(lane < half)[None, :]
    c = jnp.where(in_rot, cos[:, idx], 1.0)
    s = jnp.where(first, -sin[:, idx], jnp.where(in_rot, sin[:, idx], 0.0))
    return c, s


def _dsa_kernel(qi_ref, kilo_ref, kihi_ref, wi_ref, q_ref, k_ref, v_ref, o_ref, key_scr, bias_scr, *,
                q_pos0, l_valid, n_sel):
    tq = q_ref.shape[0]
    lp = k_ref.shape[0]
    dn = (((1,), (1,)), ((), ()))

    kilo = kilo_ref[...]
    kihi = kihi_ref[...]
    wi = wi_ref[...] * ((IDX_HEADS * IDX_DIM) ** -0.5)
    score = jnp.zeros((tq, lp), F32)
    for j in range(IDX_HEADS // 2):
        qi2 = qi_ref[:, j * LANES:(j + 1) * LANES]
        s_lo = lax.dot_general(qi2, kilo, dn, preferred_element_type=F32)
        s_hi = lax.dot_general(qi2, kihi, dn, preferred_element_type=F32)
        c = IDX_DIM + 2 * j
        score = score + jnp.maximum(s_lo, 0.0) * wi[:, c:c + 1]
        score = score + jnp.maximum(s_hi, 0.0) * wi[:, c + 1:c + 2]

    q_pos = q_pos0 + pl.program_id(1) * tq + lax.broadcasted_iota(jnp.int32, (tq, lp), 0)
    k_pos = lax.broadcasted_iota(jnp.int32, (tq, lp), 1)
    shift = CHUNK.bit_length() - 1
    adm = jnp.logical_and(jnp.right_shift(k_pos, shift) <= jnp.right_shift(q_pos, shift), k_pos < l_valid)
    bits = pltpu.bitcast(score, jnp.int32)
    key = jnp.where(bits < 0, jnp.bitwise_xor(bits, jnp.int32(0x7FFFFFFF)), bits)
    key_scr[...] = jnp.where(adm, key, jnp.int32(INT_MIN))

    def count_ge(t):
        return jnp.sum(jnp.where(key_scr[...] >= t, 1.0, 0.0), axis=-1, keepdims=True)

    zero = jnp.zeros((tq, 1), jnp.int32)
    t0 = jnp.where(count_ge(zero) >= n_sel, zero, jnp.int32(INT_MIN))

    def body(i, t):
        cand = jnp.bitwise_or(t, jnp.left_shift(jnp.int32(1), 30 - i))
        return jnp.where(count_ge(cand) >= n_sel, cand, t)

    thr = lax.fori_loop(0, 31, body, t0)
    keys = key_scr[...]
    bias_scr[...] = jnp.where(jnp.logical_and(keys >= thr, adm), 0.0, -jnp.inf)

    need = n_sel - jnp.sum(jnp.where(keys > thr, 1.0, 0.0), axis=-1, keepdims=True)
    n_tied = jnp.sum(jnp.where(keys == thr, 1.0, 0.0), axis=-1, keepdims=True)
    finite_thr = thr > jnp.int32(INT_MIN)
    over = jnp.logical_and(n_tied > need, finite_thr)

    @pl.when(jnp.max(jnp.where(over, 1.0, 0.0)) > 0.0)
    def _():
        tri = jnp.where(lax.broadcasted_iota(jnp.int32, (LANES, LANES), 0)
                        <= lax.broadcasted_iota(jnp.int32, (LANES, LANES), 1), 1.0, 0.0).astype(BF16)
        carry = jnp.zeros((tq, 1), F32)
        for c in range(lp // LANES):
            cols = slice(c * LANES, (c + 1) * LANES)
            k_c = key_scr[:, cols]
            tied = jnp.where(k_c == thr, 1.0, 0.0)
            rank = jnp.dot(tied.astype(BF16), tri, preferred_element_type=F32) + carry
            keep_tie = jnp.logical_and(jnp.logical_and(k_c == thr, rank <= need), finite_thr)
            bias_scr[:, cols] = jnp.where(jnp.logical_or(k_c > thr, keep_tie), 0.0, -jnp.inf)
            carry = carry + jnp.sum(tied, axis=-1, keepdims=True)

    group = A_HEADS // A_KV_HEADS
    for kvh in range(A_KV_HEADS):
        kk = k_ref[:, kvh * HEAD_DIM:(kvh + 1) * HEAD_DIM]
        vv = v_ref[:, kvh * HEAD_DIM:(kvh + 1) * HEAD_DIM]
        for g in range(group):
            cols = slice((kvh * group + g) * HEAD_DIM, (kvh * group + g + 1) * HEAD_DIM)
            s = lax.dot_general(q_ref[:, cols], kk, dn, preferred_element_type=F32) + bias_scr[...]
            m = jnp.max(s, axis=-1, keepdims=True)
            p = jnp.exp(s - m)
            l = jnp.sum(p, axis=-1, keepdims=True)
            o = jnp.dot(p.astype(BF16), vv, preferred_element_type=F32) / l
            o_ref[:, cols] = o.astype(o_ref.dtype)


def _dsa_attention(qi, ki_lo, ki_hi, z, q, k, v, *, n_batch, t, tq, row0, q_stride, lp, q_pos0, l_valid, n_sel):
    nq = t // tq
    qrow = lambda bi, i: (row0 + bi * q_stride) // tq + i
    qmap = lambda bi, i: (qrow(bi, i), 0)
    kmap = lambda bi, i: (bi, 0, 0)
    return pl.pallas_call(
        functools.partial(_dsa_kernel, q_pos0=q_pos0, l_valid=l_valid, n_sel=n_sel),
        grid=(n_batch, nq),
        in_specs=[pl.BlockSpec((tq, qi.shape[1]), qmap),
                  pl.BlockSpec((None, lp, LANES), kmap),
                  pl.BlockSpec((None, lp, LANES), kmap),
                  pl.BlockSpec((tq, LANES), lambda bi, i: (qrow(bi, i), AB_KW // LANES)),
                  pl.BlockSpec((tq, q.shape[1]), qmap),
                  pl.BlockSpec((None, lp, k.shape[2]), kmap),
                  pl.BlockSpec((None, lp, v.shape[2]), kmap)],
        out_specs=pl.BlockSpec((tq, q.shape[1]), lambda bi, i: (bi * nq + i, 0)),
        out_shape=jax.ShapeDtypeStruct((n_batch * t, q.shape[1]), BF16),
        scratch_shapes=[pltpu.VMEM((tq, lp), jnp.int32), pltpu.VMEM((tq, lp), F32)],
        compiler_params=_cparams(("parallel", "arbitrary")),
        name="dsa_attention",
    )(qi, ki_lo, ki_hi, z, q, k, v)


def _conv_kernel(bg_ref, cg_ref, u_ref, st_ref, w_ref, y_ref, ns_ref):
    t = u_ref.shape[0]
    u = cg_ref[...] * u_ref[...]
    st = st_ref[...]
    row = lax.broadcasted_iota(jnp.int32, u.shape, 0)
    u1 = jnp.where(row == 0, st[1:2, :], pltpu.roll(u, 1, 0))
    u2 = jnp.where(row == 0, st[0:1, :], jnp.where(row == 1, st[1:2, :], pltpu.roll(u, 2, 0)))
    w = w_ref[...]
    y = u2 * w[0:1, :] + u1 * w[1:2, :] + u * w[2:3, :]
    y_ref[...] = (bg_ref[...] * y).astype(y_ref.dtype)
    ns_ref[...] = u[t - (CONV_W - 1):t, :]


def _short_conv(z, state, w, *, n_batch, t, row0):
    tc = 256
    nb = row0 // t
    col = lambda c0: pl.BlockSpec((t, tc), lambda bi, j: (nb + bi, c0 // tc + j))
    return pl.pallas_call(
        _conv_kernel,
        grid=(n_batch, B_WIDTH // tc),
        in_specs=[col(AB_BG), col(AB_CG), col(AB_BIN),
                  pl.BlockSpec((None, CONV_W - 1, tc), lambda bi, j: (bi, 0, j)),
                  pl.BlockSpec((CONV_W, tc), lambda bi, j: (0, j))],
        out_specs=[pl.BlockSpec((t, tc), lambda bi, j: (bi, j)),
                   pl.BlockSpec((None, CONV_W - 1, tc), lambda bi, j: (bi, 0, j))],
        out_shape=[jax.ShapeDtypeStruct((n_batch * t, B_WIDTH), BF16),
                   jax.ShapeDtypeStruct((n_batch, CONV_W - 1, B_WIDTH), F32)],
        compiler_params=_cparams(("parallel", "parallel")),
        name="short_conv",
    )(z, z, z, state, w)


def _cumsum_rows(x):
    n = x.shape[0]
    row = lax.broadcasted_iota(jnp.int32, x.shape, 0)
    s = 1
    while s < n:
        x = x + jnp.where(row >= s, pltpu.roll(x, s, 0), 0.0)
        s *= 2
    return x


def _bcast_rows(x, period, offset):
    n, w = x.shape
    return jnp.concatenate(
        [jnp.broadcast_to(x[b * period + offset:b * period + offset + 1, :], (period, w)) for b in range(n // period)],
        axis=0)


def _hgrn_kernel(q_ref, f_ref, v_ref, g_ref, lb_ref, gain_ref, s0_ref, o_ref, s_out_ref, st_scr, *, tc):
    step = pl.program_id(2)
    tb = q_ref.shape[0]
    nt = (((1,), (1,)), ((), ()))
    tn = (((0,), (0,)), ((), ()))

    @pl.when(step == 0)
    def _():
        st_scr[...] = s0_ref[...].T

    lbv = lb_ref[...]
    row = lax.broadcasted_iota(jnp.int32, (tc, C_DK), 0)
    trow = lax.broadcasted_iota(jnp.int32, (tc, tc), 0)
    tcol = lax.broadcasted_iota(jnp.int32, (tc, tc), 1)
    levels = [m for m in (32, 16, 8) if 2 * m <= tc]

    def chunk(c, carry):
        rows = pl.ds(pl.multiple_of(c * tc, tc), tc)
        f = lbv + (1.0 - lbv) * jax.nn.sigmoid(f_ref[rows, :])
        kk = 1.0 - f
        a = _cumsum_rows(jnp.log(f))
        qr = q_ref[rows, :]
        q = qr * jax.nn.sigmoid(qr)
        v = v_ref[rows, :].astype(BF16)
        st = st_scr[...]

        o = lax.dot_general((q * jnp.exp(a)).astype(BF16), st.astype(BF16), nt, preferred_element_type=F32)

        sc = jnp.zeros((tc, tc), F32)
        for m in levels:
            ref = _bcast_rows(a, 2 * m, m - 1)
            upper = jnp.bitwise_and(row, 2 * m - 1) >= m
            e = jnp.exp(jnp.where(upper, a - ref, ref - a))
            qm = jnp.where(upper, q * e, 0.0).astype(BF16)
            km = jnp.where(upper, 0.0, kk * e).astype(BF16)
            sm = lax.dot_general(qm, km, nt, preferred_element_type=F32)
            shift = (2 * m).bit_length() - 1
            same = jnp.right_shift(trow, shift) == jnp.right_shift(tcol, shift)
            sc = sc + jnp.where(same, sm, 0.0)
        for s in range(SUBLANES):
            ok = jnp.bitwise_and(row, SUBLANES - 1) >= s
            d = jnp.where(ok, a - _bcast_rows(a, SUBLANES, s), 0.0)
            w = q * _bcast_rows(kk, SUBLANES, s) * jnp.exp(d)
            col = jnp.sum(w, axis=-1, keepdims=True)
            here = jnp.logical_and(tcol == jnp.bitwise_and(trow, -SUBLANES) + s,
                                   jnp.bitwise_and(trow, SUBLANES - 1) >= s)
            sc = sc + jnp.where(here, col, 0.0)

        o = o + jnp.dot(sc.astype(BF16), v, preferred_element_type=F32)

        a_last = a[tc - 1:tc, :]
        kd = (kk * jnp.exp(a_last - a)).astype(BF16)
        st_scr[...] = st * jnp.exp(a_last) + lax.dot_general(v, kd, tn, preferred_element_type=F32)

        ms = jnp.mean(o * o, axis=-1, keepdims=True)
        gr = g_ref[rows, :]
        y = o * lax.rsqrt(ms + NORM_EPS) * gain_ref[...] * (gr * jax.nn.sigmoid(gr))
        o_ref[rows, :] = y.astype(o_ref.dtype)
        return carry

    lax.fori_loop(0, tb // tc, chunk, 0, unroll=min(HGRN_UNROLL, tb // tc))

    @pl.when(step == pl.num_programs(2) - 1)
    def _():
        s_out_ref[...] = st_scr[...].T


def _hgrn2(z, lb, gain, s0, *, n_batch, t, row0):
    tc = min(CHUNK, t)
    tb = min(512, t)
    nt = t // tb
    hb = CD_W // C_DK
    xspec = lambda seg: pl.BlockSpec((tb, C_DK), lambda bi, h, i: (row0 // tb + bi * nt + i, seg * hb + h))
    smap = lambda bi, h, i: (bi, h, 0, 0)
    return pl.pallas_call(
        functools.partial(_hgrn_kernel, tc=tc),
        grid=(n_batch, C_HEADS, nt),
        in_specs=[xspec(0), xspec(1), xspec(2), xspec(3),
                  pl.BlockSpec((1, C_DK), lambda bi, h, i: (0, h)),
                  pl.BlockSpec((1, C_DV), lambda bi, h, i: (0, 0)),
                  pl.BlockSpec((None, None, C_DK, C_DV), smap)],
        out_specs=[pl.BlockSpec((tb, C_DV), lambda bi, h, i: (bi * nt + i, h)),
                   pl.BlockSpec((None, None, C_DK, C_DV), smap)],
        out_shape=[jax.ShapeDtypeStruct((n_batch * t, CD_W), BF16), jax.ShapeDtypeStruct(s0.shape, F32)],
        scratch_shapes=[pltpu.VMEM((C_DV, C_DK), F32)],
        compiler_params=_cparams(("parallel", "parallel", "arbitrary")),
        name="hgrn2",
    )(z, z, z, z, lb, gain, s0)


def _cumsum_kernel(x_ref, o_ref):
    o_ref[...] = _cumsum_rows(x_ref[...])


def _cumsum_time(x):
    b, l, w = x.shape
    return pl.pallas_call(
        _cumsum_kernel,
        grid=(b,),
        in_specs=[pl.BlockSpec((None, l, w), lambda i: (i, 0, 0))],
        out_specs=pl.BlockSpec((None, l, w), lambda i: (i, 0, 0)),
        out_shape=jax.ShapeDtypeStruct(x.shape, F32),
        compiler_params=_cparams(("parallel",)),
        name="cumsum_time",
    )(x)


def _prep_d_kernel(zq_ref, zk_ref, zf_ref, gq_ref, gk_ref, fb_ref, q_ref, k_ref, kb_ref, lf_ref):
    for h in range(D_HEADS):
        cols = slice(h * HEAD_DIM, (h + 1) * HEAD_DIM)
        q_ref[:, cols] = (_head_rms(zq_ref[:, cols], gq_ref[...]) * (HEAD_DIM ** -0.5)).astype(BF16)
        y = _head_rms(zk_ref[:, cols], gk_ref[...])
        k_ref[:, cols] = y
        kb_ref[:, cols] = y.astype(BF16)
    x = zf_ref[:, 0:LANES] + fb_ref[...]
    lf_ref[...] = jnp.minimum(x, 0.0) - jnp.log(1.0 + jnp.exp(-jnp.abs(x)))


def _prep_d(za, zb, gq, gk, fb):
    m = za.shape[0]
    tm = _pick_tile(m, (256, 128, 96, 64, 32, 16))
    row = lambda w, c: pl.BlockSpec((tm, w), lambda i: (i, c))
    full = lambda w: pl.BlockSpec((tm, w), lambda i: (i, 0))
    par = pl.BlockSpec((1, LANES), lambda i: (0, 0))
    sds = lambda w, dt: jax.ShapeDtypeStruct((m, w), dt)
    return pl.pallas_call(
        _prep_d_kernel,
        grid=(m // tm,),
        in_specs=[row(CD_W, 4), row(CD_W, 5), row(N_ALIGN, CD_FD // N_ALIGN), par, par, par],
        out_specs=[full(CD_W), full(CD_W), full(CD_W), full(LANES)],
        out_shape=[sds(CD_W, BF16), sds(CD_W, F32), sds(CD_W, BF16), sds(LANES, F32)],
        compiler_params=_cparams(("parallel",)),
        name="prep_d",
    )(za, za, zb, gq, gk, fb)


def _fox_kernel(q_ref, k_ref, v_ref, cq_ref, ck_ref, g_ref, o_ref, *, tq):
    t = q_ref.shape[0]
    dn = (((1,), (1,)), ((), ()))
    row = lax.broadcasted_iota(jnp.int32, (tq, tq), 0)
    col = lax.broadcasted_iota(jnp.int32, (tq, tq), 1)
    for qi in range(t // tq):
        lo, hi = qi * tq, (qi + 1) * tq
        q = q_ref[lo:hi, :]
        cq = cq_ref[lo:hi, :]
        s_d = lax.dot_general(q, k_ref[lo:hi, :], dn, preferred_element_type=F32) + cq - ck_ref[:, lo:hi]
        s_d = jnp.where(col <= row, s_d, -jnp.inf)
        m = jnp.max(s_d, axis=-1, keepdims=True)
        if qi:
            s_o = lax.dot_general(q, k_ref[0:lo, :], dn, preferred_element_type=F32) + cq - ck_ref[:, 0:lo]
            m = jnp.maximum(m, jnp.max(s_o, axis=-1, keepdims=True))
        p_d = jnp.exp(s_d - m)
        l = jnp.sum(p_d, axis=-1, keepdims=True)
        acc = jnp.dot(p_d.astype(BF16), v_ref[lo:hi, :].astype(BF16), preferred_element_type=F32)
        if qi:
            p_o = jnp.exp(s_o - m)
            l = l + jnp.sum(p_o, axis=-1, keepdims=True)
            acc = acc + jnp.dot(p_o.astype(BF16), v_ref[0:lo, :].astype(BF16), preferred_element_type=F32)
        o = acc / l * jax.nn.sigmoid(g_ref[lo:hi, :])
        o_ref[lo:hi, :] = o.astype(o_ref.dtype)


def _fox_prefill(q, k, za, zb, cum_col, cum_row, *, n_batch, t):
    hb = CD_W // HEAD_DIM
    tq = min(FOX_TQ, t)
    qmap = lambda bi, h: (bi, h)
    return pl.pallas_call(
        functools.partial(_fox_kernel, tq=tq),
        grid=(n_batch, D_HEADS),
        in_specs=[pl.BlockSpec((t, HEAD_DIM), qmap),
                  pl.BlockSpec((t, HEAD_DIM), qmap),
                  pl.BlockSpec((t, HEAD_DIM), lambda bi, h: (bi, 6 * hb + h)),
                  pl.BlockSpec((None, None, t, 1), lambda bi, h: (bi, h, 0, 0)),
                  pl.BlockSpec((None, None, 1, t), lambda bi, h: (bi, h, 0, 0)),
                  pl.BlockSpec((t, HEAD_DIM), lambda bi, h: (bi, h))],
        out_specs=pl.BlockSpec((t, HEAD_DIM), qmap),
        out_shape=jax.ShapeDtypeStruct((n_batch * t, CD_W), BF16),
        compiler_params=_cparams(("parallel", "parallel")),
        name="fox_prefill",
    )(q, k, za, cum_col, cum_row, zb)


def _fox_dec_kernel(q_ref, kc_ref, vc_ref, kn_ref, vn_ref, cq_ref, ckc_ref, ckn_ref, g_ref, o_ref,
                    m_scr, l_scr, acc_scr):
    j = pl.program_id(1)
    rows = q_ref.shape[0]
    hmask = D_HEADS - 1
    hshift = D_HEADS.bit_length() - 1
    dn = (((1,), (1,)), ((), ()))
    q = q_ref[...]
    cq = cq_ref[...]

    def scores(k, ck, causal):
        s = lax.dot_general(q, k, dn, preferred_element_type=F32) + cq - ck
        r = lax.broadcasted_iota(jnp.int32, s.shape, 0)
        c = lax.broadcasted_iota(jnp.int32, s.shape, 1)
        ok = jnp.bitwise_and(r, hmask) == jnp.bitwise_and(c, hmask)
        if causal:
            ok = jnp.logical_and(ok, jnp.right_shift(c, hshift) <= jnp.right_shift(r, hshift))
        return jnp.where(ok, s, -jnp.inf)

    @pl.when(j == 0)
    def _():
        s = scores(kn_ref[...], ckn_ref[...], True)
        m = jnp.max(s, axis=-1, keepdims=True)
        p = jnp.exp(s - m)
        m_scr[...] = m
        l_scr[...] = jnp.sum(p, axis=-1, keepdims=True)
        acc_scr[...] = jnp.dot(p.astype(BF16), vn_ref[...].astype(BF16), preferred_element_type=F32)

    s = scores(kc_ref[...].astype(BF16), ckc_ref[...], False)
    m_prev = m_scr[...]
    m_new = jnp.maximum(m_prev, jnp.max(s, axis=-1, keepdims=True))
    alpha = jnp.exp(m_prev - m_new)
    p = jnp.exp(s - m_new)
    l_scr[...] = alpha * l_scr[...] + jnp.sum(p, axis=-1, keepdims=True)
    acc_scr[...] = alpha * acc_scr[...] + jnp.dot(p.astype(BF16), vc_ref[...].astype(BF16),
                                                  preferred_element_type=F32)
    m_scr[...] = m_new

    @pl.when(j == pl.num_programs(1) - 1)
    def _():
        o = acc_scr[...] / l_scr[...] * jax.nn.sigmoid(g_ref[...])
        o_ref[...] = o.astype(o_ref.dtype)


def _fox_decode(q, k_new, v_new, g, k_cache, v_cache, cq_col, ck_cache_row, ck_new_row):
    b, rows, _ = q.shape
    pc = k_cache.shape[1]
    blk = min(FOX_DEC_POS * D_HEADS, pc)
    new = pl.BlockSpec((None, rows, HEAD_DIM), lambda bi, j: (bi, 0, 0))
    cache = pl.BlockSpec((None, blk, HEAD_DIM), lambda bi, j: (bi, j, 0))
    return pl.pallas_call(
        _fox_dec_kernel,
        grid=(b, pc // blk),
        in_specs=[new, cache, cache, new, new,
                  pl.BlockSpec((None, rows, 1), lambda bi, j: (bi, 0, 0)),
                  pl.BlockSpec((None, 1, blk), lambda bi, j: (bi, 0, j)),
                  pl.BlockSpec((None, 1, rows), lambda bi, j: (bi, 0, 0)),
                  new],
        out_specs=new,
        out_shape=jax.ShapeDtypeStruct((b, rows, HEAD_DIM), BF16),
        scratch_shapes=[pltpu.VMEM((rows, 1), F32), pltpu.VMEM((rows, 1), F32), pltpu.VMEM((rows, HEAD_DIM), F32)],
        compiler_params=_cparams(("parallel", "arbitrary")),
        name="fox_decode",
    )(q, k_cache, v_cache, k_new, v_new, cq_col, ck_cache_row, ck_new_row, g)


def _pad_cols(w, n):
    return jnp.pad(w, ((0, 0), (0, n - w.shape[1])))


def _ab_tail_weight(w):
    o = np.cumsum((0,) + AB_SIZES)
    return jnp.concatenate([w[:, o[6]:].astype(BF16), _pad_cols(w[:, o[4]:o[6]].astype(BF16), N_ALIGN)], axis=1)


def _cd_tail_weight(w):
    o = np.cumsum((0,) + CD_SIZES)
    return jnp.concatenate([w[:, o[8]:].astype(BF16), _pad_cols(w[:, o[7]:o[8]].astype(BF16), N_ALIGN)], axis=1)


def _expand_groups(mod_p, mod_s, seq_p):
    return jnp.concatenate([jnp.repeat(mod_p, seq_p // ROW_GROUP, axis=0), mod_s], axis=0)


def _lane_row(v):
    return jnp.pad(v.astype(F32), (0, LANES - v.shape[0]))[None, :]


def kernel(x_prompt, x_sample, c_prompt, c_sample, cache_a_k, cache_a_v, cache_a_idx_k, state_b_conv, state_c_s, cache_d_k, cache_d_v, cache_d_logf, ada_w, ada_b, norm_mix, norm_ffn, ab_w_in, ab_w_out, a_q_norm, a_k_norm, b_conv_w, cd_w_in, cd_w_out, c_lb, c_o_norm, d_q_norm, d_k_norm, d_f_bias, ffn_w_gate, ffn_w_up, ffn_w_down):
    bp, tp, d = x_prompt.shape
    bs, ts, _ = x_sample.shape
    past = cache_a_k.shape[2]
    depth = ada_w.shape[0]
    assert ts == ROW_GROUP and tp % ROW_GROUP == 0 and d == D_MODEL
    mp, ms = bp * tp, bs * ts

    x = jnp.concatenate([x_prompt.reshape(mp, d), x_sample.reshape(ms, d)], axis=0)
    c_all = jnp.concatenate([c_prompt, c_sample], axis=0)
    c_rows = -(-c_all.shape[0] // SUBLANES) * SUBLANES
    c_pad = jnp.pad(c_all, ((0, c_rows - c_all.shape[0]), (0, 0)))

    lb_all = jnp.cumsum(jax.nn.softmax(c_lb.astype(F32), axis=0), axis=0)
    lb_all = lb_all - lb_all[0]

    pos = jnp.concatenate([jnp.tile(jnp.arange(tp), bp), past + jnp.tile(jnp.arange(ts), bs)])
    cos_q, sin_q = _rope_tables(pos, HEAD_DIM)
    cos_i, sin_i = _rope_tables(pos, IDX_DIM)

    outs_p, outs_s = {}, {}
    mod_all = _ada_mod(c_pad, ada_w, ada_b[:, None, :])
    w_down = ffn_w_down.astype(BF16)
    SH1, SC1, G1, SH2, SC2, G2 = range(6)
    for l in range(depth):
        g16 = _expand_groups(mod_all[l, :bp], mod_all[l, bp:bp + bs], tp)
        h = _norm_mod(x, norm_mix[l][None, :], g16, SC1, SH1)
        if l % 2 == 0:
            e = l // 2
            za = _matmul_head(h, ab_w_in, e, AB_HEAD)
            z = _matmul(h, _ab_tail_weight(ab_w_in[e]))
            q_b, qi_b, k_f, k_b, v_b, ki_f, ki_lo, ki_hi = _prep_a(
                za, z, cos_q, sin_q, cos_i, sin_i, a_q_norm[e][None, :], a_k_norm[e][None, :])
            v_f = za[:, AB_V:AB_V + A_KV_HEADS * HEAD_DIM]

            keys_p = [a[:mp].reshape(bp, tp, -1) for a in (ki_lo, ki_hi, k_b, v_b)]
            grp = DSA_GROUP if tp % DSA_GROUP == 0 else tp
            ya_groups = []
            for g0 in range(0, tp, grp):
                o = _dsa_attention(qi_b, keys_p[0], keys_p[1], z, q_b, keys_p[2], keys_p[3], n_batch=bp, t=grp,
                                   tq=min(DSA_TQ, grp), row0=g0, q_stride=tp, lp=g0 + grp, q_pos0=g0,
                                   l_valid=g0 + grp, n_sel=min(TOPK_MAX, tp // 4))
                ya_groups.append(o.reshape(bp, grp, -1))
            ya_p = jnp.concatenate(ya_groups, axis=1).reshape(mp, -1)
            ls = past + ts
            lp = -(-ls // LANES) * LANES

            def with_cache(cache, new_rows):
                cache = cache.reshape(bs, past, -1).astype(BF16)
                new_rows = new_rows[mp:].reshape(bs, ts, -1)
                full = jnp.concatenate([cache, new_rows], axis=1)
                return jnp.pad(full, ((0, 0), (0, lp - ls), (0, 0)))

            zi = jnp.zeros_like(cache_a_idx_k[e])
            ya_s = _dsa_attention(
                qi_b, with_cache(jnp.concatenate([cache_a_idx_k[e], zi], -1), ki_lo),
                with_cache(jnp.concatenate([zi, cache_a_idx_k[e]], -1), ki_hi), z, q_b,
                with_cache(cache_a_k[e], k_b), with_cache(cache_a_v[e], v_b),
                n_batch=bs, t=ts, tq=ts, row0=mp, q_stride=ts, lp=lp, q_pos0=past, l_valid=ls,
                n_sel=min(TOPK_MAX, ls // 4))

            yb_p, conv_p = _short_conv(z, jnp.zeros((bp, CONV_W - 1, B_WIDTH), F32), b_conv_w[e],
                                       n_batch=bp, t=tp, row0=0)
            yb_s, conv_s = _short_conv(z, state_b_conv[e], b_conv_w[e], n_batch=bs, t=ts, row0=mp)
            x = _mix_out(jnp.concatenate([ya_p, ya_s], axis=0), jnp.concatenate([yb_p, yb_s], axis=0),
                         ab_w_out[e].astype(BF16), x, g16, G1)

            for od, sl, bn, t, conv_new in ((outs_p, slice(0, mp), bp, tp, conv_p),
                                            (outs_s, slice(mp, mp + ms), bs, ts, conv_s)):
                od.setdefault("a_k", []).append(k_f[sl].reshape(bn, t, A_KV_HEADS, HEAD_DIM))
                od.setdefault("a_v", []).append(v_f[sl].reshape(bn, t, A_KV_HEADS, HEAD_DIM))
                od.setdefault("a_ik", []).append(ki_f[sl].reshape(bn, t, IDX_DIM))
                od.setdefault("b_conv", []).append(conv_new)
        else:
            od_ = l // 2
            za = _matmul_head(h, cd_w_in, od_, CD_HEAD)
            z = _matmul(h, _cd_tail_weight(cd_w_in[od_]))
            q_b, k_f, k_b, lf_slab = _prep_d(za, z, d_q_norm[od_][None, :], d_k_norm[od_][None, :],
                                             _lane_row(d_f_bias[od_]))
            logf = lf_slab[:, :D_HEADS]
            v_f = za[:, 6 * CD_W:7 * CD_W]
            lb = lb_all[l][None, :]
            gain = c_o_norm[od_][None, :]

            yc_p, s_p = _hgrn2(za, lb, gain, jnp.zeros((bp, C_HEADS, C_DK, C_DV), F32), n_batch=bp, t=tp, row0=0)
            yc_s, s_s = _hgrn2(za, lb, gain, state_c_s[od_], n_batch=bs, t=ts, row0=mp)

            cum_p = _cumsum_time(lf_slab[:mp].reshape(bp, tp, LANES))[:, :, :D_HEADS].transpose(0, 2, 1)
            yd_p = _fox_prefill(q_b, k_b, za, z, cum_p[..., None], cum_p[:, :, None, :], n_batch=bp, t=tp)
            lf_cache = jnp.pad(cache_d_logf[od_], ((0, 0), (0, 0), (0, LANES - D_HEADS)))
            cum_s = _cumsum_time(jnp.concatenate([lf_cache, lf_slab[mp:].reshape(bs, ts, LANES)], axis=1))
            cum_s = cum_s[:, :, :D_HEADS]
            by_head = lambda a: a.reshape(bs, -1, HEAD_DIM)
            yd_s = _fox_decode(by_head(q_b[mp:]), by_head(k_b[mp:]), by_head(v_f[mp:]), by_head(z[mp:, :CD_W]),
                               by_head(cache_d_k[od_]), by_head(cache_d_v[od_]),
                               cum_s[:, past:].reshape(bs, ts * D_HEADS, 1),
                               cum_s[:, :past].reshape(bs, 1, past * D_HEADS),
                               cum_s[:, past:].reshape(bs, 1, ts * D_HEADS)).reshape(ms, CD_W)
            x = _mix_out(jnp.concatenate([yc_p, yc_s], axis=0), jnp.concatenate([yd_p, yd_s], axis=0),
                         cd_w_out[od_].astype(BF16), x, g16, G1)

            for od, sl, bn, t, s_new in ((outs_p, slice(0, mp), bp, tp, s_p),
                                         (outs_s, slice(mp, mp + ms), bs, ts, s_s)):
                od.setdefault("c_s", []).append(s_new)
                od.setdefault("d_k", []).append(k_f[sl].reshape(bn, t, D_HEADS, HEAD_DIM))
                od.setdefault("d_v", []).append(v_f[sl].reshape(bn, t, D_HEADS, HEAD_DIM))
                od.setdefault("d_lf", []).append(logf[sl].reshape(bn, t, D_HEADS))
        h2 = _norm_mod(x, norm_ffn[l][None, :], g16, SC2, SH2)
        hid = _ffn_gate_up(h2, ffn_w_gate, ffn_w_up, l)
        x = _ffn_down(hid, w_down, l, x, g16, G2)

    y_prompt = x[:mp].reshape(bp, tp, d)
    y_sample = x[mp:].reshape(bs, ts, d)
    names = ("a_k", "a_v", "a_ik", "b_conv", "c_s", "d_k", "d_v", "d_lf")
    return ((y_prompt, y_sample) + tuple(jnp.stack(outs_p[n]) for n in names)
            + tuple(jnp.stack(outs_s[n]) for n in names))
```
